```python
import functools
import jax
import jax.numpy as jnp
from jax import lax
import numpy as np

D_MODEL = 1024
BATCH = 8
SEQ = 2048
DEPTH = 4
DEC_BATCH = 128
DEC_SEQ = 1
PAST_LEN = 2048
PAGE_SIZE = 128

MIX_WIDTH = D_MODEL // 2
FOX_HEAD_DIM = 64
FOX_HEADS = MIX_WIDTH // FOX_HEAD_DIM
Q_BLOCK = 128
POOL_WINDOWS = (2, 4, 8, 16)
POOL_GROUP = MIX_WIDTH // len(POOL_WINDOWS)
POOL_BUF = max(POOL_WINDOWS) - 1
GDN_DK = 128
GDN_DV = 128
GDN_HEADS = MIX_WIDTH // GDN_DV
GDN_QKV = GDN_HEADS * (2 * GDN_DK + GDN_DV)
CONV_W = 4
GDN_CHUNK = 64
N_BRANCH = 3
N_EXPERTS = 32
TOP_K = 4
D_FF = D_MODEL
SWIGLU_LIMIT = 7.0
SWIGLU_ALPHA = 1.702
MOE_BLOCK = 128
PLE_DIM = 256
DEEPNORM_ALPHA = (2 * DEPTH) ** 0.25
DEEPNORM_BETA = (8 * DEPTH) ** -0.25
LN_EPS = 1e-5
RMS_EPS = 1e-6
L2_EPS = 1e-6

IN_SIZES = (MIX_WIDTH, MIX_WIDTH, MIX_WIDTH, FOX_HEADS,
            MIX_WIDTH,
            GDN_QKV, GDN_HEADS, GDN_HEADS, MIX_WIDTH,
            N_BRANCH * D_MODEL)
IN_SPLITS = [int(s) for s in np.cumsum(IN_SIZES)[:-1]]
D_IN = int(sum(IN_SIZES))

kernel_name = 'hybrid_fox_pool_gdn_moe_step'


def _layer_norm(x, g, b):
    xf = x.astype(jnp.float32)
    mu = jnp.mean(xf, -1, keepdims=True)
    var = jnp.mean(jnp.square(xf - mu), -1, keepdims=True)
    return ((xf - mu) * lax.rsqrt(var + LN_EPS) * g.astype(jnp.float32) + b.astype(jnp.float32)).astype(x.dtype)


def _l2norm(x):
    xf = x.astype(jnp.float32)
    return xf * lax.rsqrt(jnp.sum(xf * xf, -1, keepdims=True) + L2_EPS)


def _gated_rms(o, z, w):
    o = o * lax.rsqrt(jnp.mean(o * o, -1, keepdims=True) + RMS_EPS)
    return o * w.astype(jnp.float32) * jax.nn.silu(z.astype(jnp.float32))


def _fox_prompt(q, k, v, logf):
    B, S, H, hd = q.shape
    nb = S // Q_BLOCK
    c = jnp.cumsum(logf, axis=1).transpose(0, 2, 1)
    qb = jnp.swapaxes(q.reshape(B, nb, Q_BLOCK, H, hd), 0, 1)
    cb = c.reshape(B, H, nb, Q_BLOCK).transpose(2, 0, 1, 3)
    kpos = jnp.arange(S)
    scale = hd ** -0.5

    def block(args):
        qi, ci, i = args
        s = jnp.einsum('bqhd,bkhd->bhqk', qi, k, preferred_element_type=jnp.float32) * scale
        s = s + ci[..., :, None] - c[:, :, None, :]
        qpos = i * Q_BLOCK + jnp.arange(Q_BLOCK)
        s = jnp.where(kpos[None, :] <= qpos[:, None], s, -jnp.inf)
        p = jax.nn.softmax(s, axis=-1)
        return jnp.einsum('bhqk,bkhd->bqhd', p.astype(v.dtype), v)

    o = lax.map(block, (qb, cb, jnp.arange(nb)))
    return jnp.swapaxes(o, 0, 1).reshape(B, S, H, hd)


def _fox_decode(q, k, v, logf, k_past, v_past, logf_past):
    T = q.shape[1]
    P = k_past.shape[1]
    scale = FOX_HEAD_DIM ** -0.5
    cn = jnp.cumsum(logf, axis=1).transpose(0, 2, 1)
    lp = logf_past.astype(jnp.float32)
    rp = (lax.cumsum(lp, axis=1, reverse=True) - lp).transpose(0, 2, 1)
    s_past = (jnp.einsum('bqhd,bkhd->bhqk', q, k_past.astype(q.dtype), preferred_element_type=jnp.float32) * scale
              + rp[:, :, None, :] + cn[:, :, :, None])
    s_new = (jnp.einsum('bqhd,bkhd->bhqk', q, k, preferred_element_type=jnp.float32) * scale
             + cn[:, :, :, None] - cn[:, :, None, :])
    s_new = jnp.where(jnp.tril(jnp.ones((T, T), bool)), s_new, -jnp.inf)
    p = jax.nn.softmax(jnp.concatenate([s_past, s_new], axis=-1), axis=-1)
    o = jnp.einsum('bhqk,bkhd->bqhd', p[..., :P].astype(v.dtype), v_past.astype(v.dtype))
    return o + jnp.einsum('bhqk,bkhd->bqhd', p[..., P:].astype(v.dtype), v)


def _pool_mix(ub, buf, pos0, w_grp, scale):
    B, T, _ = ub.shape
    xc = jnp.concatenate([buf.astype(jnp.float32), ub.astype(jnp.float32)], axis=1)
    cs = jnp.concatenate([jnp.zeros((B, 1, MIX_WIDTH), jnp.float32), jnp.cumsum(xc, axis=1)], axis=1)
    pos = (pos0 + jnp.arange(T)).astype(jnp.float32)
    means = []
    for gi, w in enumerate(POOL_WINDOWS):
        c = cs[..., gi * POOL_GROUP:(gi + 1) * POOL_GROUP]
        win = c[:, POOL_BUF + 1:POOL_BUF + 1 + T] - c[:, POOL_BUF + 1 - w:POOL_BUF + 1 - w + T]
        means.append(win / jnp.minimum(pos + 1.0, float(w))[None, :, None])
    z = jnp.concatenate(means, axis=-1) - ub.astype(jnp.float32)
    z = jnp.einsum('btgc,gcd->btgd', z.reshape(B, T, len(POOL_WINDOWS), POOL_GROUP), w_grp.astype(jnp.float32))
    return (z.reshape(B, T, MIX_WIDTH) * scale.astype(jnp.float32)).astype(ub.dtype)


def _short_conv(x, buf, w):
    T = x.shape[1]
    xc = jnp.concatenate([buf.astype(x.dtype), x], axis=1)
    y = xc[:, 0:T] * w[0]
    for j in range(1, CONV_W):
        y = y + xc[:, j:j + T] * w[j]
    return jax.nn.silu(y), xc[:, -(CONV_W - 1):]


def _gdn_chunked(q, k, v, g, beta, s0):
    B, T, H, dk = q.shape
    dv = v.shape[-1]
    C = GDN_CHUNK
    n = T // C

    def chunks(a):
        return jnp.moveaxis(a.reshape((B, n, C) + a.shape[2:]), 3, 1)

    q = chunks(q) * dk ** -0.5
    k, v, g, beta = chunks(k), chunks(v), chunks(g), chunks(beta)
    decay = jnp.cumsum(g, axis=-1)
    incl = jnp.tril(jnp.ones((C, C), bool))
    strict = jnp.tril(jnp.ones((C, C), bool), -1)
    gam = jnp.exp(jnp.where(incl, decay[..., :, None] - decay[..., None, :], -jnp.inf))
    kb = k * beta[..., None]
    m = jnp.where(strict, jnp.einsum('...ik,...jk->...ij', kb, k) * gam, 0.0)
    a = m + jnp.eye(C, dtype=m.dtype)
    rhs = jnp.concatenate([v * beta[..., None], kb * jnp.exp(decay)[..., None]], axis=-1)
    sol = lax.linalg.triangular_solve(a, rhs, left_side=True, lower=True, unit_diagonal=True)
    value, kcum = sol[..., :dv], sol[..., dv:]
    qk = jnp.einsum('...ik,...jk->...ij', q, k) * gam
    qd = q * jnp.exp(decay)[..., None]
    kd = k * jnp.exp(decay[..., -1:] - decay)[..., None]
    last = jnp.exp(decay[..., -1])

    def step(S, xs):
        value_i, kcum_i, qk_i, qd_i, kd_i, last_i = xs
        vnew = value_i - jnp.einsum('bhck,bhkv->bhcv', kcum_i, S)
        o = jnp.einsum('bhck,bhkv->bhcv', qd_i, S) + jnp.einsum('bhcj,bhjv->bhcv', qk_i, vnew)
        S = S * last_i[..., None, None] + jnp.einsum('bhck,bhcv->bhkv', kd_i, vnew)
        return S, o

    xs = (jnp.moveaxis(value, 2, 0), jnp.moveaxis(kcum, 2, 0), jnp.moveaxis(qk, 2, 0),
          jnp.moveaxis(qd, 2, 0), jnp.moveaxis(kd, 2, 0), jnp.moveaxis(last, 2, 0))
    S, o = lax.scan(step, s0, xs)
    o = jnp.moveaxis(jnp.moveaxis(o, 0, 2), 1, 3)
    return o.reshape(B, T, H, dv), S


def _gdn_recurrent(q, k, v, g, beta, s0):
    scale = q.shape[-1] ** -0.5

    def step(S, xs):
        qt, kt, vt, gt, bt = xs
        S = S * jnp.exp(gt)[..., None, None]
        u = (vt - jnp.einsum('bhk,bhkv->bhv', kt, S)) * bt[..., None]
        S = S + jnp.einsum('bhk,bhv->bhkv', kt, u)
        return S, jnp.einsum('bhk,bhkv->bhv', qt * scale, S)

    xs = (jnp.swapaxes(q, 0, 1), jnp.swapaxes(k, 0, 1), jnp.swapaxes(v, 0, 1),
          jnp.swapaxes(g, 0, 1), jnp.swapaxes(beta, 0, 1))
    S, o = lax.scan(step, s0, xs)
    return jnp.swapaxes(o, 0, 1), S


def _moe(x, router_w, router_b, w_up, b_up, w_down, b_down):
    T, D = x.shape
    logits = (x @ router_w).astype(jnp.float32) + router_b.astype(jnp.float32)
    top_v, top_e = lax.top_k(logits, TOP_K)
    gate = jax.nn.softmax(top_v, axis=-1)
    TK = T * TOP_K
    flat_e = top_e.reshape(-1)
    flat_tok = jnp.arange(TK, dtype=jnp.int32) // TOP_K
    order = jnp.argsort(flat_e)
    se = flat_e[order]
    counts = jnp.bincount(flat_e, length=N_EXPERTS)
    pcounts = (counts + MOE_BLOCK - 1) // MOE_BLOCK * MOE_BLOCK
    start = jnp.cumsum(counts) - counts
    pend = jnp.cumsum(pcounts)
    pstart = pend - pcounts
    dest = pstart[se] + jnp.arange(TK, dtype=jnp.int32) - start[se]
    n_rows = -(-(TK + N_EXPERTS * (MOE_BLOCK - 1)) // MOE_BLOCK) * MOE_BLOCK
    n_blk = n_rows // MOE_BLOCK
    row_tok = jnp.full((n_rows,), T, jnp.int32).at[dest].set(flat_tok[order])
    blk_e = jnp.minimum(jnp.searchsorted(pend, jnp.arange(n_blk, dtype=pend.dtype) * MOE_BLOCK, side='right'),
                        N_EXPERTS - 1)
    xpad = jnp.concatenate([x, jnp.zeros((1, D), x.dtype)], axis=0)

    def expert_block(args):
        toks, e = args
        hb = xpad[toks] @ w_up[e] + b_up[e]
        gt, up = jnp.split(hb, 2, axis=-1)
        gt = jnp.minimum(gt, SWIGLU_LIMIT)
        up = jnp.clip(up, -SWIGLU_LIMIT, SWIGLU_LIMIT)
        return ((up + 1.0) * (gt * jax.nn.sigmoid(SWIGLU_ALPHA * gt))) @ w_down[e] + b_down[e]

    rows = lax.map(expert_block, (row_tok.reshape(n_blk, MOE_BLOCK), blk_e)).reshape(n_rows, D)
    contrib = rows[dest] * gate.reshape(-1)[order][:, None].astype(rows.dtype)
    return jax.ops.segment_sum(contrib, flat_tok[order], num_segments=T).astype(x.dtype)


def _layer(x, pin, lw, attend, pool_buf, pos0, conv_buf, delta):
    (w_in, b_f, w_grp, pool_scale, conv_w, a_log, dt_bias, onorm_w, w_branch, w_o,
     ln1_g, ln1_b, router_w, router_b, w_up, b_up, w_down, b_down, w_pe, w_peg, ln2_g, ln2_b) = lw
    B, T, _ = x.shape
    f32 = jnp.float32
    qa, ka, va, fa, ub, qkv_c, a_c, b_c, z_c, gl = jnp.split(x @ w_in, IN_SPLITS, axis=-1)
    qa = qa.reshape(B, T, FOX_HEADS, FOX_HEAD_DIM)
    ka = ka.reshape(B, T, FOX_HEADS, FOX_HEAD_DIM)
    va = va.reshape(B, T, FOX_HEADS, FOX_HEAD_DIM)
    logf = jax.nn.log_sigmoid(fa.astype(f32) + b_f.astype(f32))
    o_a = attend(qa, ka, va, logf).reshape(B, T, MIX_WIDTH).astype(x.dtype)
    o_b = _pool_mix(ub, pool_buf, pos0, w_grp, pool_scale).astype(x.dtype)
    new_pool = jnp.concatenate([pool_buf.astype(ub.dtype), ub], axis=1)[:, -POOL_BUF:]
    qkv, new_conv = _short_conv(qkv_c, conv_buf, conv_w)
    qc, kc, vc = jnp.split(qkv, [GDN_HEADS * GDN_DK, 2 * GDN_HEADS * GDN_DK], axis=-1)
    qc = _l2norm(qc.reshape(B, T, GDN_HEADS, GDN_DK))
    kc = _l2norm(kc.reshape(B, T, GDN_HEADS, GDN_DK))
    vc = vc.reshape(B, T, GDN_HEADS, GDN_DV).astype(f32)
    beta = jax.nn.sigmoid(b_c.astype(f32))
    g = -jnp.exp(a_log.astype(f32)) * jax.nn.softplus(a_c.astype(f32) + dt_bias.astype(f32))
    o_c, s_new = delta(qc, kc, vc, g, beta)
    o_c = _gated_rms(o_c, z_c.reshape(B, T, GDN_HEADS, GDN_DV), onorm_w).reshape(B, T, MIX_WIDTH).astype(x.dtype)
    br = jnp.stack([o_a, o_b, o_c], axis=2)
    proj = jnp.einsum('btiw,iwd->btid', br, w_branch)
    m = jnp.sum(jax.nn.sigmoid(gl.reshape(B, T, N_BRANCH, D_MODEL)) * proj, axis=2)
    x = _layer_norm(DEEPNORM_ALPHA * x + m @ w_o, ln1_g, ln1_b)
    f = _moe(x.reshape(B * T, D_MODEL), router_w, router_b, w_up, b_up, w_down, b_down).reshape(B, T, D_MODEL)
    h = DEEPNORM_ALPHA * x + f
    h = h + jax.nn.sigmoid(h @ w_peg) * (pin.astype(h.dtype) @ w_pe)
    x = _layer_norm(h, ln2_g, ln2_b)
    return x, (ka, va, logf, new_pool, new_conv, s_new)


def setup_inputs(seed: int = 0) -> dict:
    key = jax.random.key(seed)
    ks = iter(jax.random.split(key, 48))

    def nrm(shape, scale):
        return jax.random.normal(next(ks), shape, jnp.float32) * scale

    n_pages = PAST_LEN // PAGE_SIZE
    n_used = DEC_BATCH * n_pages
    n_phys = n_used + n_used // 4
    d = D_MODEL
    inp = {}
    inp['x_prompt'] = nrm((BATCH, SEQ, d), 1.0)
    inp['x_sample'] = nrm((DEC_BATCH, DEC_SEQ, d), 1.0)
    inp['cache_k'] = nrm((DEPTH, n_phys, PAGE_SIZE, FOX_HEADS, FOX_HEAD_DIM), 1.0)
    inp['cache_v'] = nrm((DEPTH, n_phys, PAGE_SIZE, FOX_HEADS, FOX_HEAD_DIM), 1.0)
    inp['cache_logf'] = jax.nn.log_sigmoid(nrm((DEPTH, n_phys, PAGE_SIZE, FOX_HEADS), 1.0) + 4.0)
    inp['state_pool'] = nrm((DEPTH, DEC_BATCH, POOL_BUF, MIX_WIDTH), 1.0)
    inp['state_conv'] = nrm((DEPTH, DEC_BATCH, CONV_W - 1, GDN_QKV), 1.0)
    inp['state_delta'] = nrm((DEPTH, DEC_BATCH, GDN_HEADS, GDN_DK, GDN_DV), 0.1)
    inp['page_table'] = jax.random.permutation(next(ks), n_phys)[:n_used].reshape(DEC_BATCH, n_pages).astype(jnp.int32)
    inp['p_prompt'] = nrm((DEPTH, BATCH, SEQ, PLE_DIM), 1.0)
    inp['p_sample'] = nrm((DEPTH, DEC_BATCH, DEC_SEQ, PLE_DIM), 1.0)
    inp['w_in'] = nrm((DEPTH, d, D_IN), d ** -0.5)
    inp['b_f'] = jnp.linspace(2.0, 7.0, FOX_HEADS, dtype=jnp.float32)[None, :] + nrm((DEPTH, FOX_HEADS), 0.1)
    inp['w_grp'] = nrm((DEPTH, len(POOL_WINDOWS), POOL_GROUP, POOL_GROUP), POOL_GROUP ** -0.5)
    inp['pool_scale'] = 1.0 + nrm((DEPTH, MIX_WIDTH), 0.1)
    inp['conv_w'] = nrm((DEPTH, CONV_W, GDN_QKV), CONV_W ** -0.5)
    inp['a_log'] = jnp.log(jax.random.uniform(next(ks), (DEPTH, GDN_HEADS), jnp.float32, 1.0, 16.0))
    dt = jnp.exp(jax.random.uniform(next(ks), (DEPTH, GDN_HEADS), jnp.float32, np.log(1e-3), np.log(1e-1)))
    inp['dt_bias'] = dt + jnp.log(-jnp.expm1(-dt))
    inp['onorm_w'] = 1.0 + nrm((DEPTH, GDN_DV), 0.1)
    inp['w_branch'] = nrm((DEPTH, N_BRANCH, MIX_WIDTH, d), MIX_WIDTH ** -0.5)
    inp['w_o'] = nrm((DEPTH, d, d), d ** -0.5 * DEEPNORM_BETA)
    inp['ln1_g'] = 1.0 + nrm((DEPTH, d), 0.05)
    inp['ln1_b'] = nrm((DEPTH, d), 0.02)
    inp['router_w'] = nrm((DEPTH, d, N_EXPERTS), d ** -0.5)
    inp['router_b'] = nrm((DEPTH, N_EXPERTS), 0.01)
    inp['w_up'] = nrm((DEPTH, N_EXPERTS, d, 2 * D_FF), d ** -0.5)
    inp['b_up'] = nrm((DEPTH, N_EXPERTS, 2 * D_FF), 0.02)
    inp['w_down'] = nrm((DEPTH, N_EXPERTS, D_FF, d), D_FF ** -0.5 * DEEPNORM_BETA)
    inp['b_down'] = nrm((DEPTH, N_EXPERTS, d), 0.02)
    inp['w_pe'] = nrm((DEPTH, PLE_DIM, d), PLE_DIM ** -0.5)
    inp['w_peg'] = nrm((DEPTH, d, d), d ** -0.5)
    inp['ln2_g'] = 1.0 + nrm((DEPTH, d), 0.05)
    inp['ln2_b'] = nrm((DEPTH, d), 0.02)
    return inp


def reference(x_prompt, x_sample, cache_k, cache_v, cache_logf, state_pool, state_conv, state_delta,
              page_table, p_prompt, p_sample, w_in, b_f, w_grp, pool_scale, conv_w, a_log, dt_bias,
              onorm_w, w_branch, w_o, ln1_g, ln1_b, router_w, router_b, w_up, b_up, w_down, b_down,
              w_pe, w_peg, ln2_g, ln2_b):
    layer_params = (w_in, b_f, w_grp, pool_scale, conv_w, a_log, dt_bias, onorm_w, w_branch, w_o,
                    ln1_g, ln1_b, router_w, router_b, w_up, b_up, w_down, b_down, w_pe, w_peg, ln2_g, ln2_b)
    B = x_prompt.shape[0]
    Bd = x_sample.shape[0]
    past = page_table.shape[1] * PAGE_SIZE
    pool0 = jnp.zeros((B, POOL_BUF, MIX_WIDTH), x_prompt.dtype)
    conv0 = jnp.zeros((B, CONV_W - 1, GDN_QKV), x_prompt.dtype)
    gdn_prompt = functools.partial(_gdn_chunked, s0=jnp.zeros((B, GDN_HEADS, GDN_DK, GDN_DV), jnp.float32))
    xp, xs = x_prompt, x_sample
    rows_p, rows_s = [], []
    for l in range(DEPTH):
        lw = [a[l] for a in layer_params]
        xp, st_p = _layer(xp, p_prompt[l], lw, _fox_prompt, pool0, 0, conv0, gdn_prompt)
        rows_p.append(st_p)
        attend = functools.partial(
            _fox_decode,
            k_past=cache_k[l, page_table].reshape(Bd, past, FOX_HEADS, FOX_HEAD_DIM),
            v_past=cache_v[l, page_table].reshape(Bd, past, FOX_HEADS, FOX_HEAD_DIM),
            logf_past=cache_logf[l, page_table].reshape(Bd, past, FOX_HEADS))
        gdn_sample = functools.partial(_gdn_recurrent, s0=state_delta[l].astype(jnp.float32))
        xs, st_s = _layer(xs, p_sample[l], lw, attend, state_pool[l], past, state_conv[l], gdn_sample)
        rows_s.append(st_s)

    def stk(rows, i):
        return jnp.stack([r[i] for r in rows])

    return (xp, xs,
            stk(rows_p, 0), stk(rows_p, 1), stk(rows_p, 2), stk(rows_p, 3), stk(rows_p, 4), stk(rows_p, 5),
            stk(rows_s, 0), stk(rows_s, 1), stk(rows_s, 2), stk(rows_s, 3), stk(rows_s, 4), stk(rows_s, 5))
```

```python
import functools

import jax
import jax.numpy as jnp
from jax import lax
from jax.experimental import pallas as pl
from jax.experimental.pallas import tpu as pltpu

F32 = jnp.float32
BF16 = jnp.bfloat16

POOL_WINDOWS = (2, 4, 8, 16)
CONV_W = 4
GDN_CHUNK = 64
TOP_K = 4
SWIGLU_LIMIT = 7.0
SWIGLU_ALPHA = 1.702
LN_EPS = 1e-5
RMS_EPS = 1e-6
L2_EPS = 1e-6
MOE_BLOCK = 128

LANES = 128
SUBLANES = 8
VMEM_LIMIT = 56 * 1024 * 1024

NEG_BIG = -1e30


def _cparams(sem):
    return pltpu.CompilerParams(dimension_semantics=sem, vmem_limit_bytes=VMEM_LIMIT)


def _split3(a):
    hi = a.astype(BF16)
    r1 = a - hi.astype(F32)
    mid = r1.astype(BF16)
    lo = (r1 - mid.astype(F32)).astype(BF16)
    return hi, mid, lo


def _dot(a, b):
    return jnp.dot(a, b, preferred_element_type=F32)


def _dot_nt(a, b):
    return lax.dot_general(a, b, (((1,), (1,)), ((), ())), preferred_element_type=F32)


def _dot_tn(a, b):
    return lax.dot_general(a, b, (((0,), (0,)), ((), ())), preferred_element_type=F32)


def _dot_exact_rhs01(a, m01):
    hi, mid, lo = _split3(a)
    return _dot(hi, m01) + _dot(mid, m01) + _dot(lo, m01)


def _dot_exact_lhs01(m01, a):
    hi, mid, lo = _split3(a)
    return _dot(m01, hi) + _dot(m01, mid) + _dot(m01, lo)


def _dot_hi(a, b):
    ah = a.astype(BF16)
    al = (a - ah.astype(F32)).astype(BF16)
    bh = b.astype(BF16)
    bl = (b - bh.astype(F32)).astype(BF16)
    return _dot(ah, bh) + _dot(ah, bl) + _dot(al, bh)


def _sigmoid(x):
    return 1.0 / (1.0 + jnp.exp(-x))


def _softplus(x):
    return jnp.maximum(x, 0.0) + jnp.log1p(jnp.exp(-jnp.abs(x)))


def _silu(x):
    return x * _sigmoid(x)


def _layer_norm(h, g, b):
    mu = jnp.mean(h, axis=-1, keepdims=True)
    d = h - mu
    var = jnp.mean(d * d, axis=-1, keepdims=True)
    return d * lax.rsqrt(var + LN_EPS) * g + b


def _mm_kernel(x_ref, w_ref, o_ref, xb_ref):
    @pl.when(pl.program_id(1) == 0)
    def _():
        xb_ref[...] = x_ref[...].astype(BF16)

    o_ref[...] = _dot(xb_ref[...], w_ref[...])


def _matmul(x, w, tm, tn):
    m, k = x.shape
    n = w.shape[1]
    return pl.pallas_call(
        _mm_kernel,
        grid=(m // tm, n // tn),
        in_specs=[pl.BlockSpec((tm, k), lambda i, j: (i, 0)),
                  pl.BlockSpec((k, tn), lambda i, j: (0, j))],
        out_specs=pl.BlockSpec((tm, tn), lambda i, j: (i, j)),
        out_shape=jax.ShapeDtypeStruct((m, n), F32),
        scratch_shapes=[pltpu.VMEM((tm, k), BF16)],
        compiler_params=_cparams(("parallel", "arbitrary")),
        name="proj",
    )(x, w)


def _gate_act(s, bias, alog, idx, n_f, n_g):
    z = s + bias
    logf = jnp.minimum(z, 0.0) - jnp.log1p(jnp.exp(-jnp.abs(z)))
    g = -jnp.exp(alog) * _softplus(z)
    beta = _sigmoid(z)
    return jnp.where(idx < n_f, logf, jnp.where(idx < n_f + n_g, g, beta))


def _small_kernel(x_ref, w_ref, wt_ref, bias_ref, alog_ref, biast_ref, alogt_ref, o_ref, ot_ref, *, n_f, n_g):
    xb = x_ref[...].astype(BF16)
    s = _dot(xb, w_ref[...])
    st = _dot_nt(wt_ref[...], xb)
    lane = lax.broadcasted_iota(jnp.int32, s.shape, 1)
    o_ref[...] = _gate_act(s, bias_ref[...], alog_ref[...], lane, n_f, n_g)
    row = lax.broadcasted_iota(jnp.int32, st.shape, 0)
    ot_ref[...] = _gate_act(st, biast_ref[...], alogt_ref[...], row, n_f, n_g)


def _small_proj(x, w_small, bias, alog, n_f, n_g, tm):
    m, k = x.shape
    nc = w_small.shape[1]
    w_pad = jnp.zeros((k, LANES), BF16).at[:, :nc].set(w_small.astype(BF16))
    wt = jnp.transpose(w_small).astype(BF16)
    bias_pad = jnp.zeros((1, LANES), F32).at[0, :nc].set(bias)
    alog_pad = jnp.zeros((1, LANES), F32).at[0, :nc].set(alog)
    kern = functools.partial(_small_kernel, n_f=n_f, n_g=n_g)
    full = lambda shape: pl.BlockSpec(shape, lambda i: (0,) * len(shape))
    return pl.pallas_call(
        kern,
        grid=(m // tm,),
        in_specs=[pl.BlockSpec((tm, k), lambda i: (i, 0)), full((k, LANES)), full((nc, k)),
                  full((1, LANES)), full((1, LANES)), full((nc, 1)), full((nc, 1))],
        out_specs=[pl.BlockSpec((tm, LANES), lambda i: (i, 0)), pl.BlockSpec((nc, tm), lambda i: (0, i))],
        out_shape=[jax.ShapeDtypeStruct((m, LANES), F32), jax.ShapeDtypeStruct((nc, m), F32)],
        compiler_params=_cparams(("parallel",)),
        name="gate_cols",
    )(x, w_pad, wt, bias_pad, alog_pad, bias.reshape(nc, 1), alog.reshape(nc, 1))


CUMSUM_CHUNK = 256


def _cumsum_kernel(gt_ref, ct_ref):
    n = gt_ref.shape[1] // CUMSUM_CHUNK
    r = lax.broadcasted_iota(jnp.int32, (CUMSUM_CHUNK, CUMSUM_CHUNK), 0)
    c = lax.broadcasted_iota(jnp.int32, (CUMSUM_CHUNK, CUMSUM_CHUNK), 1)
    triu = jnp.where(r <= c, 1.0, 0.0).astype(BF16)
    carry = jnp.zeros((gt_ref.shape[0], 1), F32)
    for i in range(n):
        sl = pl.ds(i * CUMSUM_CHUNK, CUMSUM_CHUNK)
        cs = _dot_exact_rhs01(gt_ref[:, sl], triu) + carry
        ct_ref[:, sl] = cs
        carry = cs[:, CUMSUM_CHUNK - 1:CUMSUM_CHUNK]


def _cumsum_rows(gt, n_seq, seq):
    nc = gt.shape[0]
    return pl.pallas_call(
        _cumsum_kernel,
        grid=(n_seq,),
        in_specs=[pl.BlockSpec((nc, seq), lambda b: (0, b))],
        out_specs=pl.BlockSpec((nc, seq), lambda b: (0, b)),
        out_shape=jax.ShapeDtypeStruct((nc, n_seq * seq), F32),
        compiler_params=_cparams(("parallel",)),
        name="logf_cumsum",
    )(gt)


def _fox_kernel(q_ref, k_ref, v_ref, ct_ref, o_ref, m_ref, l_ref, acc_ref, *, n_heads, head_dim):
    qi = pl.program_id(1)
    ki = pl.program_id(2)
    tq = q_ref.shape[0]
    tk = k_ref.shape[0]
    n_pairs = n_heads // 2
    scale = head_dim ** -0.5

    @pl.when(ki == 0)
    def _():
        m_ref[...] = jnp.full(m_ref.shape, NEG_BIG, F32)
        l_ref[...] = jnp.zeros(l_ref.shape, F32)
        acc_ref[...] = jnp.zeros(acc_ref.shape, F32)

    def step(masked):
        lane = lax.broadcasted_iota(jnp.int32, (tq, LANES), 1)
        low = lane < head_dim
        if masked:
            rq = lax.broadcasted_iota(jnp.int32, (tq, tk), 0)
            ck = lax.broadcasted_iota(jnp.int32, (tq, tk), 1)
            causal = ck <= rq
        for p in range(n_pairs):
            sl = pl.ds(p * LANES, LANES)
            qp = (q_ref[:, sl] * scale).astype(BF16)
            kp = k_ref[:, sl].astype(BF16)
            vp = v_ref[:, sl].astype(BF16)
            pv = []
            alphas = []
            for half in range(2):
                h = 2 * p + half
                qh = jnp.where(low if half == 0 else jnp.logical_not(low), qp, jnp.zeros_like(qp))
                s = _dot_nt(qh, kp) - ct_ref[h:h + 1, :]
                if masked:
                    s = jnp.where(causal, s, NEG_BIG)
                m_prev = m_ref[h]
                m_new = jnp.maximum(m_prev, jnp.max(s, axis=-1, keepdims=True))
                alpha = jnp.exp(m_prev - m_new)
                pr = jnp.exp(s - m_new[:, 0:1])
                l_ref[h] = alpha * l_ref[h] + jnp.sum(pr, axis=-1, keepdims=True)
                m_ref[h] = m_new
                pv.append(_dot(pr.astype(BF16), vp))
                alphas.append(alpha)
            a = jnp.where(low, alphas[0], alphas[1])
            acc_ref[p] = a * acc_ref[p] + jnp.where(low, pv[0], pv[1])

    @pl.when(ki < qi)
    def _():
        step(False)

    @pl.when(ki == qi)
    def _():
        step(True)
        lane = lax.broadcasted_iota(jnp.int32, (tq, LANES), 1)
        low = lane < head_dim
        for p in range(n_pairs):
            l = jnp.where(low, l_ref[2 * p], l_ref[2 * p + 1])
            o_ref[:, pl.ds(p * LANES, LANES)] = acc_ref[p] / l


def _fox_prompt(qkv, ct, n_rows_out, batch, seq, n_heads, head_dim, tq):
    w = n_heads * head_dim
    nq = seq // tq
    kern = functools.partial(_fox_kernel, n_heads=n_heads, head_dim=head_dim)
    return pl.pallas_call(
        kern,
        grid=(batch, nq, nq),
        in_specs=[pl.BlockSpec((tq, w), lambda b, i, j: (b * nq + i, 0)),
                  pl.BlockSpec((tq, w), lambda b, i, j: (b * nq + jnp.minimum(i, j), 1)),
                  pl.BlockSpec((tq, w), lambda b, i, j: (b * nq + jnp.minimum(i, j), 2)),
                  pl.BlockSpec((n_heads, tq), lambda b, i, j: (0, b * nq + jnp.minimum(i, j)))],
        out_specs=pl.BlockSpec((tq, w), lambda b, i, j: (b * nq + i, 0)),
        out_shape=jax.ShapeDtypeStruct((n_rows_out, w), F32),
        scratch_shapes=[pltpu.VMEM((n_heads, tq, LANES), F32), pltpu.VMEM((n_heads, tq, LANES), F32),
                        pltpu.VMEM((n_heads // 2, tq, LANES), F32)],
        compiler_params=_cparams(("parallel", "parallel", "arbitrary")),
        name="fox_prompt",
    )(qkv, qkv, qkv, ct)


def _pool_kernel(x_ref, halo_ref, w_ref, scale_ref, o_ref, buf_ref):
    i = pl.program_id(1)
    tb = x_ref.shape[0]
    hal = max(POOL_WINDOWS)
    buf_ref[0:hal, :] = jnp.where(i == 0, 0.0, halo_ref[...])
    buf_ref[hal:, :] = x_ref[...]
    pos = (i * tb + lax.broadcasted_iota(jnp.int32, (tb, 1), 0) + 1).astype(F32)
    for g, w in enumerate(POOL_WINDOWS):
        sl = pl.ds(g * LANES, LANES)
        win = buf_ref[hal:hal + tb, sl]
        for j in range(1, w):
            win = win + buf_ref[hal - j:hal - j + tb, sl]
        z = win / jnp.minimum(pos, float(w)) - x_ref[:, sl]
        o_ref[:, sl] = _dot(z.astype(BF16), w_ref[g]) * scale_ref[:, sl]


def _pool_prompt(ub, w_grp, scale, n_rows_out, batch, seq, tb):
    width = ub.shape[1]
    hal = max(POOL_WINDOWS)
    nb = seq // tb
    return pl.pallas_call(
        _pool_kernel,
        grid=(batch, nb),
        in_specs=[pl.BlockSpec((tb, width), lambda b, i: (b * nb + i, 0)),
                  pl.BlockSpec((hal, width), lambda b, i: (jnp.maximum((b * nb + i) * (tb // hal) - 1, 0), 0)),
                  pl.BlockSpec(w_grp.shape, lambda b, i: (0, 0, 0)),
                  pl.BlockSpec((1, width), lambda b, i: (0, 0))],
        out_specs=pl.BlockSpec((tb, width), lambda b, i: (b * nb + i, 0)),
        out_shape=jax.ShapeDtypeStruct((n_rows_out, width), F32),
        scratch_shapes=[pltpu.VMEM((hal + tb, width), F32)],
        compiler_params=_cparams(("parallel", "parallel")),
        name="pool_prompt",
    )(ub, ub, w_grp, scale)


def _gdn_kernel(x_ref, halo_ref, z_ref, gcol_ref, gt_ref, cw_ref, nw_ref, o_ref, s_out_ref, s_ref, buf_ref,
                *, n_heads, dk, dv, col_g, col_beta):
    i = pl.program_id(1)
    tb = x_ref.shape[0]
    c = GDN_CHUNK
    hal = SUBLANES

    @pl.when(i == 0)
    def _():
        s_ref[...] = jnp.zeros(s_ref.shape, F32)

    buf_ref[0:hal, :] = jnp.where(i == 0, 0.0, halo_ref[...])
    buf_ref[hal:, :] = x_ref[...]
    y = buf_ref[hal:hal + tb, :] * cw_ref[CONV_W - 1:CONV_W, :]
    for j in range(CONV_W - 1):
        off = hal - (CONV_W - 1) + j
        y = y + buf_ref[off:off + tb, :] * cw_ref[j:j + 1, :]
    qkv = _silu(y)

    r = lax.broadcasted_iota(jnp.int32, (c, c), 0)
    cc = lax.broadcasted_iota(jnp.int32, (c, c), 1)
    incl = cc <= r
    strict = cc < r
    tril01 = jnp.where(incl, 1.0, 0.0).astype(BF16)
    triu01 = jnp.where(r <= cc, 1.0, 0.0).astype(BF16)
    n_steps = max(1, (c - 1).bit_length())

    for ch in range(tb // c):
        rows = pl.ds(ch * c, c)
        gcol = gcol_ref[rows, :]
        decay_col = _dot_exact_lhs01(tril01, gcol)
        decay_row = _dot_exact_rhs01(gt_ref[:, rows], triu01)
        for h in range(n_heads):
            q = qkv[ch * c:(ch + 1) * c, h * dk:(h + 1) * dk]
            k = qkv[ch * c:(ch + 1) * c, (n_heads + h) * dk:(n_heads + h + 1) * dk]
            v = qkv[ch * c:(ch + 1) * c, 2 * n_heads * dk + h * dv:2 * n_heads * dk + (h + 1) * dv]
            q = q * lax.rsqrt(jnp.sum(q * q, axis=-1, keepdims=True) + L2_EPS) * dk ** -0.5
            k = k * lax.rsqrt(jnp.sum(k * k, axis=-1, keepdims=True) + L2_EPS)
            beta = gcol[:, col_beta + h:col_beta + h + 1]
            dcol = decay_col[:, col_g + h:col_g + h + 1]
            drow = decay_row[col_g + h:col_g + h + 1, :]
            gam = jnp.exp(jnp.where(incl, dcol - drow, NEG_BIG))
            kb = k * beta
            kbf = k.astype(BF16)
            m = jnp.where(strict, _dot_nt(kb.astype(BF16), kbf) * gam, 0.0)
            qk = _dot_nt(q.astype(BF16), kbf) * gam
            ed = jnp.exp(dcol)
            x = jnp.concatenate([v * beta, kb * ed], axis=-1)
            p = -m
            for st in range(n_steps):
                x = x + _dot_hi(p, x)
                if st + 1 < n_steps:
                    p = _dot_hi(p, p)
            value = x[:, :dv]
            kcum = x[:, dv:]
            dlast = dcol[c - 1:c, :]
            qd = q * ed
            kd = k * jnp.exp(dlast - dcol)
            s = s_ref[h]
            sb = s.astype(BF16)
            vnew = value - _dot(kcum.astype(BF16), sb)
            vb = vnew.astype(BF16)
            o = _dot(qd.astype(BF16), sb) + _dot(qk.astype(BF16), vb)
            s_ref[h] = s * jnp.exp(dlast) + _dot_tn(kd.astype(BF16), vb)
            o = o * lax.rsqrt(jnp.mean(o * o, axis=-1, keepdims=True) + RMS_EPS)
            zz = z_ref[rows, pl.ds(h * dv, dv)]
            o_ref[rows, pl.ds(h * dv, dv)] = o * nw_ref[...] * _silu(zz)

    @pl.when(i == pl.num_programs(1) - 1)
    def _():
        s_out_ref[...] = s_ref[...]


def _gdn_prompt(qkvc, zc, gcol, gt, conv_w, onorm_w, n_rows_out, batch, seq, n_heads, dk, dv, col_g, col_beta, tb):
    nb = seq // tb
    wq = qkvc.shape[1]
    wz = zc.shape[1]
    nc = gt.shape[0]
    kern = functools.partial(_gdn_kernel, n_heads=n_heads, dk=dk, dv=dv, col_g=col_g, col_beta=col_beta)
    return pl.pallas_call(
        kern,
        grid=(batch, nb),
        in_specs=[pl.BlockSpec((tb, wq), lambda b, i: (b * nb + i, 0)),
                  pl.BlockSpec((SUBLANES, wq), lambda b, i: (jnp.maximum((b * nb + i) * (tb // SUBLANES) - 1, 0), 0)),
                  pl.BlockSpec((tb, wz), lambda b, i: (b * nb + i, 0)),
                  pl.BlockSpec((tb, LANES), lambda b, i: (b * nb + i, 0)),
                  pl.BlockSpec((nc, tb), lambda b, i: (0, b * nb + i)),
                  pl.BlockSpec((CONV_W, wq), lambda b, i: (0, 0)),
                  pl.BlockSpec((1, dv), lambda b, i: (0, 0))],
        out_specs=[pl.BlockSpec((tb, wz), lambda b, i: (b * nb + i, 0)),
                   pl.BlockSpec((None, n_heads, dk, dv), lambda b, i: (b, 0, 0, 0))],
        out_shape=[jax.ShapeDtypeStruct((n_rows_out, wz), F32),
                   jax.ShapeDtypeStruct((batch, n_heads, dk, dv), F32)],
        scratch_shapes=[pltpu.VMEM((n_heads, dk, dv), F32), pltpu.VMEM((SUBLANES + tb, wq), F32)],
        compiler_params=_cparams(("parallel", "arbitrary")),
        name="gdn_prompt",
    )(qkvc, qkvc, zc, gcol, gt, conv_w, onorm_w)


def _fox_decode_kernel(pt_ref, q_ref, kn_ref, vn_ref, lfn_ref, *refs, layer, n_pages, head_dim):
    lf_refs = refs[:n_pages]
    k_hbm, v_hbm, o_ref, kbuf, vbuf, sem = refs[n_pages:]
    b = pl.program_id(0)
    slot = b % 2
    _, _, page, n_heads, _ = kbuf.shape

    def start(seq, s):
        for j in range(n_pages):
            pg = pt_ref[seq * n_pages + j]
            pltpu.make_async_copy(k_hbm.at[layer, pg], kbuf.at[s, j], sem.at[0, s]).start()
            pltpu.make_async_copy(v_hbm.at[layer, pg], vbuf.at[s, j], sem.at[1, s]).start()

    @pl.when(b == 0)
    def _():
        start(0, 0)

    @pl.when(b + 1 < pl.num_programs(0))
    def _():
        start(b + 1, 1 - slot)

    pltpu.make_async_copy(k_hbm.at[layer, pl.ds(0, n_pages)], kbuf.at[slot], sem.at[0, slot]).wait()
    pltpu.make_async_copy(v_hbm.at[layer, pl.ds(0, n_pages)], vbuf.at[slot], sem.at[1, slot]).wait()

    q3 = q_ref[...] * head_dim ** -0.5
    tok3 = lax.broadcasted_iota(jnp.int32, (page, n_heads, page), 0)
    lane3 = lax.broadcasted_iota(jnp.int32, (page, n_heads, page), 2)
    eye3 = tok3 == lane3
    r = lax.broadcasted_iota(jnp.int32, (page, page), 0)
    c = lax.broadcasted_iota(jnp.int32, (page, page), 1)
    later01 = jnp.where(r > c, 1.0, 0.0).astype(BF16)

    s_new = jnp.sum(q3 * kn_ref[...], axis=-1, keepdims=True) - lfn_ref[...]
    m = s_new
    later = jnp.zeros((n_heads, 1), F32)
    scores = [None] * n_pages
    for j in reversed(range(n_pages)):
        lf = lf_refs[j][...]
        rp = _dot_exact_rhs01(lf, later01) + later
        later = later + jnp.sum(lf, axis=-1, keepdims=True)
        s3 = jnp.sum(kbuf[slot, j] * q3[None], axis=-1, keepdims=True)
        s2 = jnp.sum(jnp.where(eye3, s3, 0.0), axis=0) + rp
        scores[j] = s2
        m = jnp.maximum(m, jnp.max(s2, axis=-1, keepdims=True))
    p_new = jnp.exp(s_new - m)
    denom = p_new
    acc = p_new * vn_ref[...]
    for j in range(n_pages):
        p2 = jnp.exp(scores[j] - m)
        denom = denom + jnp.sum(p2, axis=-1, keepdims=True)
        p3 = jnp.sum(jnp.where(eye3, p2[None], 0.0), axis=-1, keepdims=True)
        acc = acc + jnp.sum(p3 * vbuf[slot, j], axis=0)
    o_ref[...] = acc / denom


def _fox_decode(layer, page_table, q3, kn3, vn3, lfn, cache_k, cache_v, lft):
    bd, n_pages = page_table.shape
    _, _, page, n_heads, head_dim = cache_k.shape
    kern = functools.partial(_fox_decode_kernel, layer=layer, n_pages=n_pages, head_dim=head_dim)
    row = pl.BlockSpec((None, n_heads, head_dim), lambda b, pt: (b, 0, 0))

    def lf_spec(j):
        return pl.BlockSpec((None, None, n_heads, page), lambda b, pt: (layer, pt[b * n_pages + j], 0, 0))

    in_specs = ([row, row, row, pl.BlockSpec((None, n_heads, 1), lambda b, pt: (b, 0, 0))]
                + [lf_spec(j) for j in range(n_pages)]
                + [pl.BlockSpec(memory_space=pl.ANY), pl.BlockSpec(memory_space=pl.ANY)])
    return pl.pallas_call(
        kern,
        grid_spec=pltpu.PrefetchScalarGridSpec(
            num_scalar_prefetch=1, grid=(bd,), in_specs=in_specs,
            out_specs=pl.BlockSpec((None, n_heads, head_dim), lambda b, pt: (b, 0, 0)),
            scratch_shapes=[pltpu.VMEM((2, n_pages, page, n_heads, head_dim), F32),
                            pltpu.VMEM((2, n_pages, page, n_heads, head_dim), F32),
                            pltpu.SemaphoreType.DMA((2, 2))]),
        out_shape=jax.ShapeDtypeStruct((bd, n_heads, head_dim), F32),
        compiler_params=_cparams(("arbitrary",)),
        name="fox_decode",
    )(page_table.reshape(-1), q3, kn3, vn3, lfn, *([lft] * n_pages), cache_k, cache_v)


def _pool_decode_kernel(st_ref, x_ref, w_ref, scale_ref, o_ref, new_ref, *, pos0):
    width = x_ref.shape[1]
    n_buf = st_ref.shape[1] // width
    for g, w in enumerate(POOL_WINDOWS):
        sl = pl.ds(g * LANES, LANES)
        win = x_ref[:, sl]
        for j in range(1, w):
            win = win + st_ref[:, pl.ds((n_buf - j) * width + g * LANES, LANES)]
        z = win / float(min(pos0 + 1, w)) - x_ref[:, sl]
        o_ref[:, sl] = _dot(z.astype(BF16), w_ref[g]) * scale_ref[:, sl]
    new_ref[:, 0:(n_buf - 1) * width] = st_ref[:, width:]
    new_ref[:, (n_buf - 1) * width:] = x_ref[...]


def _pool_decode(state_flat, x, w_grp, scale, pos0):
    bd, width = x.shape
    kern = functools.partial(_pool_decode_kernel, pos0=pos0)
    full = lambda a: pl.BlockSpec(a.shape, lambda i: (0,) * a.ndim)
    return pl.pallas_call(
        kern,
        grid=(1,),
        in_specs=[full(state_flat), full(x), full(w_grp), full(scale)],
        out_specs=[full(x), full(state_flat)],
        out_shape=[jax.ShapeDtypeStruct(x.shape, F32), jax.ShapeDtypeStruct(state_flat.shape, F32)],
        compiler_params=_cparams(("arbitrary",)),
        name="pool_decode",
    )(state_flat, x, w_grp, scale)


def _gdn_decode_kernel(x_ref, cs_ref, z_ref, g_ref, s_ref, cw_ref, nw_ref, o_ref, nc_ref, so_ref,
                       *, n_heads, dk, dv, col_g, col_beta):
    x = x_ref[...]
    cs = cs_ref[...]
    y = x * cw_ref[CONV_W - 1:CONV_W, :]
    for j in range(CONV_W - 1):
        y = y + cs[j:j + 1, :] * cw_ref[j:j + 1, :]
    qkv = _silu(y)
    nc_ref[0:CONV_W - 2, :] = cs[1:, :]
    nc_ref[CONV_W - 2:CONV_W - 1, :] = x
    gates = g_ref[...]
    for h in range(n_heads):
        q = qkv[:, h * dk:(h + 1) * dk]
        k = qkv[:, (n_heads + h) * dk:(n_heads + h + 1) * dk]
        v = qkv[:, 2 * n_heads * dk + h * dv:2 * n_heads * dk + (h + 1) * dv]
        q = q * lax.rsqrt(jnp.sum(q * q, axis=-1, keepdims=True) + L2_EPS) * dk ** -0.5
        k = k * lax.rsqrt(jnp.sum(k * k, axis=-1, keepdims=True) + L2_EPS)
        kc = jnp.transpose(jnp.broadcast_to(k, (dk, dk)))
        qc = jnp.transpose(jnp.broadcast_to(q, (dk, dk)))
        a = jnp.exp(gates[:, col_g + h:col_g + h + 1])
        beta = gates[:, col_beta + h:col_beta + h + 1]
        s = s_ref[h] * a
        u = (v - jnp.sum(kc * s, axis=0, keepdims=True)) * beta
        s = s + kc * u
        so_ref[h] = s
        o = jnp.sum(qc * s, axis=0, keepdims=True)
        o = o * lax.rsqrt(jnp.mean(o * o, axis=-1, keepdims=True) + RMS_EPS)
        o_ref[:, pl.ds(h * dv, dv)] = o * nw_ref[...] * _silu(z_ref[:, pl.ds(h * dv, dv)])


def _gdn_decode(x3, conv_state, z3, g3, s_state, conv_w, onorm_w, col_g, col_beta):
    bd, n_heads, dk, dv = s_state.shape
    wq = x3.shape[-1]
    wz = z3.shape[-1]
    kern = functools.partial(_gdn_decode_kernel, n_heads=n_heads, dk=dk, dv=dv, col_g=col_g, col_beta=col_beta)
    per_seq = lambda a: pl.BlockSpec((None,) + a.shape[1:], lambda b: (b,) + (0,) * (a.ndim - 1))
    full = lambda a: pl.BlockSpec(a.shape, lambda b: (0,) * a.ndim)
    return pl.pallas_call(
        kern,
        grid=(bd,),
        in_specs=[per_seq(x3), per_seq(conv_state), per_seq(z3), per_seq(g3), per_seq(s_state),
                  full(conv_w), full(onorm_w)],
        out_specs=[per_seq(z3), per_seq(conv_state), per_seq(s_state)],
        out_shape=[jax.ShapeDtypeStruct(z3.shape, F32), jax.ShapeDtypeStruct(conv_state.shape, F32),
                   jax.ShapeDtypeStruct(s_state.shape, F32)],
        compiler_params=_cparams(("parallel",)),
        name="gdn_decode",
    )(x3, conv_state, z3, g3, s_state, conv_w, onorm_w)


def _merge_kernel(oa_ref, ob_ref, oc_ref, gl_ref, x_ref, wb_ref, wo_ref, g_ref, b_ref, rw_ref, rb_ref,
                  x1_ref, gate_ref, idx_ref, *, alpha, n_experts):
    d = x_ref.shape[1]
    m = None
    for bi, o_ref in enumerate((oa_ref, ob_ref, oc_ref)):
        proj = _dot(o_ref[...].astype(BF16), wb_ref[bi])
        term = _sigmoid(gl_ref[:, pl.ds(bi * d, d)]) * proj
        m = term if m is None else m + term
    h = alpha * x_ref[...] + _dot(m.astype(BF16), wo_ref[...])
    x1 = _layer_norm(h, g_ref[...], b_ref[...])
    x1_ref[...] = x1
    logits = _dot_hi(x1, rw_ref[...]) + rb_ref[...]
    lane = lax.broadcasted_iota(jnp.int32, logits.shape, 1)
    lg = jnp.where(lane < n_experts, logits, NEG_BIG)
    vals = []
    gate = jnp.zeros(logits.shape, F32)
    idx = jnp.zeros(logits.shape, jnp.int32)
    for k in range(TOP_K):
        mx = jnp.max(lg, axis=-1, keepdims=True)
        ix = jnp.min(jnp.where(lg == mx, lane, LANES), axis=-1, keepdims=True)
        vals.append(mx)
        idx = jnp.where(lane == k, ix, idx)
        lg = jnp.where(lane == ix, NEG_BIG, lg)
    exps = [jnp.exp(v - vals[0]) for v in vals]
    tot = exps[0]
    for e in exps[1:]:
        tot = tot + e
    for k in range(TOP_K):
        gate = jnp.where(lane == k, exps[k] / tot, gate)
    gate_ref[...] = gate
    idx_ref[...] = idx


def _merge(oa, ob, oc, gl, x, wb, wo, ln_g, ln_b, rw, rb, alpha, n_experts, tm):
    nt, d = x.shape
    w = oa.shape[1]
    kern = functools.partial(_merge_kernel, alpha=alpha, n_experts=n_experts)
    rows = lambda width: pl.BlockSpec((tm, width), lambda i: (i, 0))
    full = lambda a: pl.BlockSpec(a.shape, lambda i: (0,) * a.ndim)
    return pl.pallas_call(
        kern,
        grid=(nt // tm,),
        in_specs=[rows(w), rows(w), rows(w), rows(gl.shape[1]), rows(d), full(wb), full(wo), full(ln_g),
                  full(ln_b), full(rw), full(rb)],
        out_specs=[rows(d), rows(LANES), rows(LANES)],
        out_shape=[jax.ShapeDtypeStruct((nt, d), F32), jax.ShapeDtypeStruct((nt, LANES), F32),
                   jax.ShapeDtypeStruct((nt, LANES), jnp.int32)],
        compiler_params=_cparams(("parallel",)),
        name="merge_ln_router",
    )(oa, ob, oc, gl, x, wb, wo, ln_g, ln_b, rw, rb)


def _moe_kernel(blk_e_ref, n_used_ref, tok_ref, tok_next_ref, dst_ref, x_hbm, wu_ref, bu_ref, wd_ref, bd_ref,
                out_hbm, xs_ref, ys_ref, gsem, ssem):
    i = pl.program_id(0)
    n_blk = pl.num_programs(0)
    n_used = n_used_ref[0]
    slot = i % 2
    blk = xs_ref.shape[1]
    dff = wd_ref.shape[0]

    def gather(idx_ref, to_slot):
        def body(r, carry):
            pltpu.make_async_copy(x_hbm.at[pl.ds(idx_ref[0, r], 1), :], xs_ref.at[to_slot, pl.ds(r, 1), :],
                                  gsem.at[to_slot]).start()
            return carry
        lax.fori_loop(0, blk, body, 0)

    def wait_gather(s):
        pltpu.make_async_copy(x_hbm.at[pl.ds(0, blk), :], xs_ref.at[s], gsem.at[s]).wait()

    def wait_scatter(s):
        pltpu.make_async_copy(ys_ref.at[s], out_hbm.at[pl.ds(0, blk), :], ssem.at[s]).wait()

    @pl.when(i == 0)
    def _():
        ys_ref[...] = jnp.zeros(ys_ref.shape, F32)
        n_real = out_hbm.shape[0] - 2 * blk
        for s in range(2):
            cp = pltpu.make_async_copy(ys_ref.at[s], out_hbm.at[pl.ds(n_real + s * blk, blk), :], ssem.at[s])
            cp.start()
            cp.wait()

    @pl.when(jnp.logical_and(i == 0, n_used > 0))
    def _():
        gather(tok_ref, 0)

    @pl.when(i + 1 < n_used)
    def _():
        gather(tok_next_ref, 1 - slot)

    @pl.when(jnp.logical_and(i >= 2, i - 2 < n_used))
    def _():
        wait_scatter(slot)

    @pl.when(i < n_used)
    def _():
        wait_gather(slot)
        xb = xs_ref[slot].astype(BF16)
        hb = _dot(xb, wu_ref[...]) + bu_ref[...]
        gt = jnp.minimum(hb[:, :dff], SWIGLU_LIMIT)
        up = jnp.clip(hb[:, dff:], -SWIGLU_LIMIT, SWIGLU_LIMIT)
        act = (up + 1.0) * (gt * _sigmoid(SWIGLU_ALPHA * gt))
        ys_ref[slot] = _dot(act.astype(BF16), wd_ref[...]) + bd_ref[...]

        def body(r, carry):
            pltpu.make_async_copy(ys_ref.at[slot, pl.ds(r, 1), :], out_hbm.at[pl.ds(dst_ref[0, r], 1), :],
                                  ssem.at[slot]).start()
            return carry
        lax.fori_loop(0, blk, body, 0)

    @pl.when(i == n_blk - 1)
    def _():
        @pl.when(i < n_used)
        def _():
            wait_scatter(slot)

        @pl.when(jnp.logical_and(i >= 1, i - 1 < n_used))
        def _():
            wait_scatter(1 - slot)


def _moe(x1, blk_e, n_used, row_tok, row_dst, wu, bu, wd, bd, n_out_rows):
    nt, d = x1.shape
    n_blk = blk_e.shape[0]
    blk = row_tok.shape[-1]
    n_exp, _, dff2 = wu.shape
    idx_spec = lambda f: pl.BlockSpec((None, 1, blk), f, memory_space=pltpu.SMEM)
    return pl.pallas_call(
        _moe_kernel,
        grid_spec=pltpu.PrefetchScalarGridSpec(
            num_scalar_prefetch=2, grid=(n_blk,),
            in_specs=[idx_spec(lambda i, be, nu: (i, 0, 0)),
                      idx_spec(lambda i, be, nu: (jnp.minimum(i + 1, n_blk - 1), 0, 0)),
                      idx_spec(lambda i, be, nu: (i, 0, 0)),
                      pl.BlockSpec(memory_space=pl.ANY),
                      pl.BlockSpec((None, d, dff2), lambda i, be, nu: (be[i], 0, 0)),
                      pl.BlockSpec((None, 1, dff2), lambda i, be, nu: (be[i], 0, 0)),
                      pl.BlockSpec((None, dff2 // 2, d), lambda i, be, nu: (be[i], 0, 0)),
                      pl.BlockSpec((None, 1, d), lambda i, be, nu: (be[i], 0, 0))],
            out_specs=pl.BlockSpec(memory_space=pl.ANY),
            scratch_shapes=[pltpu.VMEM((2, blk, d), F32), pltpu.VMEM((2, blk, d), F32),
                            pltpu.SemaphoreType.DMA((2,)), pltpu.SemaphoreType.DMA((2,))]),
        out_shape=jax.ShapeDtypeStruct((n_out_rows, d), F32),
        compiler_params=_cparams(("arbitrary",)),
        name="moe_experts",
    )(blk_e, n_used, row_tok, row_tok, row_dst, x1, wu, bu, wd, bd)


def _route(idx, n_experts, nt):
    tk = nt * TOP_K
    flat_e = idx.reshape(-1)
    onehot = (flat_e[:, None] == jnp.arange(n_experts, dtype=jnp.int32)[None, :]).astype(jnp.int32)
    csum = jnp.cumsum(onehot, axis=0)
    counts = csum[-1]
    rank = jnp.sum(onehot * csum, axis=1) - 1
    pcounts = (counts + MOE_BLOCK - 1) // MOE_BLOCK * MOE_BLOCK
    pend = jnp.cumsum(pcounts)
    pstart = pend - pcounts
    dest = pstart[flat_e] + rank
    n_rows = -(-(tk + n_experts * (MOE_BLOCK - 1)) // MOE_BLOCK) * MOE_BLOCK
    n_blk = n_rows // MOE_BLOCK
    pair = jnp.arange(tk, dtype=jnp.int32)
    row_tok = jnp.zeros((n_rows,), jnp.int32).at[dest].set(pair // TOP_K)
    rows = jnp.arange(n_rows, dtype=jnp.int32)
    dump = TOP_K * nt + (rows // MOE_BLOCK % 2) * MOE_BLOCK + rows % MOE_BLOCK
    row_dst = dump.at[dest].set((pair % TOP_K) * nt + pair // TOP_K)
    blk_e = jnp.minimum(jnp.searchsorted(pend, jnp.arange(n_blk, dtype=jnp.int32) * MOE_BLOCK, side='right'),
                        n_experts - 1).astype(jnp.int32)
    n_used = (pend[-1] // MOE_BLOCK).astype(jnp.int32).reshape(1)
    return (blk_e, n_used, row_tok.reshape(n_blk, 1, MOE_BLOCK), row_dst.reshape(n_blk, 1, MOE_BLOCK))


def _ple_kernel(x1_ref, y0_ref, y1_ref, y2_ref, y3_ref, gate_ref, p_ref, wg_ref, wp_ref, g_ref, b_ref, o_ref, *, alpha):
    gate = gate_ref[...]
    f = None
    for k, y_ref in enumerate((y0_ref, y1_ref, y2_ref, y3_ref)):
        term = gate[:, k:k + 1] * y_ref[...]
        f = term if f is None else f + term
    h = alpha * x1_ref[...] + f
    gpe = _sigmoid(_dot(h.astype(BF16), wg_ref[...])) * _dot(p_ref[...].astype(BF16), wp_ref[...])
    o_ref[...] = _layer_norm(h + gpe, g_ref[...], b_ref[...])


def _ple(x1, out4, gate, p, wg, wp, ln_g, ln_b, alpha, tm):
    nt, d = x1.shape
    nb = nt // tm
    kern = functools.partial(_ple_kernel, alpha=alpha)
    rows = lambda width: pl.BlockSpec((tm, width), lambda i: (i, 0))
    full = lambda a: pl.BlockSpec(a.shape, lambda i: (0,) * a.ndim)
    slot = lambda k: pl.BlockSpec((tm, d), lambda i: (k * nb + i, 0))
    return pl.pallas_call(
        kern,
        grid=(nb,),
        in_specs=[rows(d), slot(0), slot(1), slot(2), slot(3), rows(LANES), rows(p.shape[1]), full(wg), full(wp),
                  full(ln_g), full(ln_b)],
        out_specs=rows(d),
        out_shape=jax.ShapeDtypeStruct((nt, d), F32),
        compiler_params=_cparams(("parallel",)),
        name="combine_ple_ln",
    )(x1, out4, out4, out4, out4, gate, p, wg, wp, ln_g, ln_b)


def _row_tile(n, cap):
    best = None
    for t in range(LANES, cap + 1, LANES):
        if n % t == 0:
            best = t
    assert best is not None, n
    return best


def kernel(x_prompt, x_sample, cache_k, cache_v, cache_logf, state_pool, state_conv, state_delta, page_table,
           p_prompt, p_sample, w_in, b_f, w_grp, pool_scale, conv_w, a_log, dt_bias, onorm_w, w_branch, w_o,
           ln1_g, ln1_b, router_w, router_b, w_up, b_up, w_down, b_down, w_pe, w_peg, ln2_g, ln2_b):
    batch, seq, d = x_prompt.shape
    bd, dec_seq, _ = x_sample.shape
    assert dec_seq == 1
    depth = w_in.shape[0]
    _, _, page, n_heads, head_dim = cache_k.shape
    width = n_heads * head_dim
    _, _, gh, dk, dv = state_delta.shape
    gqkv = state_conv.shape[3]
    n_branch = w_branch.shape[1]
    n_exp = router_w.shape[2]
    ple_dim = p_prompt.shape[3]
    n_buf = state_pool.shape[2]
    past = page_table.shape[1] * page
    alpha = (2 * depth) ** 0.25
    n_p = batch * seq
    nt = n_p + bd
    tm = _row_tile(nt, 512)

    sizes = (width, width, width, n_heads, width, gqkv, gh, gh, width, n_branch * d)
    offs = [0]
    for s in sizes:
        offs.append(offs[-1] + s)
    o_q, _, _, o_f, o_ub, o_c, o_a, o_b, o_z, o_gl, o_end = offs
    assert o_end == w_in.shape[2]
    col_g, col_beta = n_heads, n_heads + gh

    x = jnp.concatenate([x_prompt.reshape(n_p, d), x_sample.reshape(bd, d)], axis=0)
    lft = jnp.swapaxes(cache_logf, 2, 3)
    rows_p, rows_s = [], []
    for l in range(depth):
        wl = w_in[l]
        w_small = jnp.concatenate([wl[:, o_f:o_f + n_heads], wl[:, o_a:o_a + gh], wl[:, o_b:o_b + gh]], axis=1)
        bias = jnp.concatenate([b_f[l], dt_bias[l], jnp.zeros((gh,), F32)])
        alog = jnp.concatenate([jnp.zeros((n_heads,), F32), a_log[l], jnp.zeros((gh,), F32)])
        qkv = _matmul(x, wl[:, o_q:o_q + 3 * width].astype(BF16), tm, width)
        ub = _matmul(x, wl[:, o_ub:o_ub + width].astype(BF16), tm, width)
        qkvc = _matmul(x, wl[:, o_c:o_c + gqkv].astype(BF16), tm, width)
        zc = _matmul(x, wl[:, o_z:o_z + width].astype(BF16), tm, width)
        gl = _matmul(x, wl[:, o_gl:].astype(BF16), tm, d)
        gcol, gt = _small_proj(x, w_small, bias, alog, n_heads, gh, tm)
        ct = _cumsum_rows(gt, batch, seq)

        wg_bf = w_grp[l].astype(BF16)
        pscale = pool_scale[l].reshape(1, width)
        onw = onorm_w[l].reshape(1, dv)
        oa = _fox_prompt(qkv, ct, n_p, batch, seq, n_heads, head_dim, min(seq, 512))
        ob = _pool_prompt(ub, wg_bf, pscale, n_p, batch, seq, min(seq, 256))
        oc, s_p = _gdn_prompt(qkvc, zc, gcol, gt, conv_w[l], onw, n_p, batch, seq, gh, dk, dv, col_g, col_beta,
                              min(seq, 2 * GDN_CHUNK))

        qs = qkv[n_p:]
        q3 = qs[:, :width].reshape(bd, n_heads, head_dim)
        kn3 = qs[:, width:2 * width].reshape(bd, n_heads, head_dim)
        vn3 = qs[:, 2 * width:].reshape(bd, n_heads, head_dim)
        lfn = jnp.transpose(gt[:n_heads, n_p:]).reshape(bd, n_heads, 1)
        oa_s = _fox_decode(l, page_table, q3, kn3, vn3, lfn, cache_k, cache_v, lft).reshape(bd, width)
        ob_s, pool_new = _pool_decode(state_pool[l].reshape(bd, n_buf * width), ub[n_p:], wg_bf, pscale, past)
        oc_s, conv_new, s_new = _gdn_decode(qkvc[n_p:].reshape(bd, 1, gqkv), state_conv[l],
                                            zc[n_p:].reshape(bd, 1, width), gcol[n_p:].reshape(bd, 1, LANES),
                                            state_delta[l], conv_w[l], onw, col_g, col_beta)
        oa = jnp.concatenate([oa, oa_s], axis=0)
        ob = jnp.concatenate([ob, ob_s], axis=0)
        oc = jnp.concatenate([oc, oc_s.reshape(bd, width)], axis=0)

        rw = jnp.zeros((d, LANES), F32).at[:, :n_exp].set(router_w[l])
        rb = jnp.zeros((1, LANES), F32).at[0, :n_exp].set(router_b[l])
        x1, gate, idx = _merge(oa, ob, oc, gl, x, w_branch[l].astype(BF16), w_o[l].astype(BF16),
                               ln1_g[l].reshape(1, d), ln1_b[l].reshape(1, d), rw, rb, alpha, n_exp, tm)
        blk_e, n_used, row_tok, row_dst = _route(idx[:, :TOP_K], n_exp, nt)
        out4 = _moe(x1, blk_e, n_used, row_tok, row_dst, w_up[l].astype(BF16), b_up[l].reshape(n_exp, 1, -1),
                    w_down[l].astype(BF16), b_down[l].reshape(n_exp, 1, -1), TOP_K * nt + 2 * MOE_BLOCK)
        p = jnp.concatenate([p_prompt[l].reshape(n_p, ple_dim), p_sample[l].reshape(bd, ple_dim)], axis=0)
        x = _ple(x1, out4, gate, p, w_peg[l].astype(BF16), w_pe[l].astype(BF16), ln2_g[l].reshape(1, d),
                 ln2_b[l].reshape(1, d), alpha, tm)

        rows_p.append((qkv[:n_p, width:2 * width].reshape(batch, seq, n_heads, head_dim),
                       qkv[:n_p, 2 * width:].reshape(batch, seq, n_heads, head_dim),
                       gcol[:n_p, :n_heads].reshape(batch, seq, n_heads),
                       ub[:n_p].reshape(batch, seq, width)[:, seq - n_buf:],
                       qkvc[:n_p].reshape(batch, seq, gqkv)[:, seq - (CONV_W - 1):],
                       s_p))
        rows_s.append((kn3.reshape(bd, 1, n_heads, head_dim), vn3.reshape(bd, 1, n_heads, head_dim),
                       gcol[n_p:, :n_heads].reshape(bd, 1, n_heads), pool_new.reshape(bd, n_buf, width),
                       conv_new, s_new))

    def stk(rows, i):
        return jnp.stack([r[i] for r in rows])

    return (x[:n_p].reshape(batch, seq, d), x[n_p:].reshape(bd, 1, d),
            stk(rows_p, 0), stk(rows_p, 1), stk(rows_p, 2), stk(rows_p, 3), stk(rows_p, 4), stk(rows_p, 5),
            stk(rows_s, 0), stk(rows_s, 1), stk(rows_s, 2), stk(rows_s, 3), stk(rows_s, 4), stk(rows_s, 5))
```

```python
import functools

import jax
import jax.numpy as jnp
from jax import lax
from jax.experimental import pallas as pl
from jax.experimental.pallas import tpu as pltpu

F32 = jnp.float32
BF16 = jnp.bfloat16

POOL_WINDOWS = (2, 4, 8, 16)
CONV_W = 4
GDN_CHUNK = 64
TOP_K = 4
SWIGLU_LIMIT = 7.0
SWIGLU_ALPHA = 1.702
LN_EPS = 1e-5
RMS_EPS = 1e-6
L2_EPS = 1e-6
MOE_BLOCK = 128

LANES = 128
SUBLANES = 8
VMEM_LIMIT = 56 * 1024 * 1024

NEG_BIG = -1e30


def _cparams(sem):
    return pltpu.CompilerParams(dimension_semantics=sem, vmem_limit_bytes=VMEM_LIMIT)


def _split3(a):
    hi = a.astype(BF16)
    r1 = a - hi.astype(F32)
    mid = r1.astype(BF16)
    lo = (r1 - mid.astype(F32)).astype(BF16)
    return hi, mid, lo


def _dot(a, b):
    return jnp.dot(a, b, preferred_element_type=F32)


def _dot_nt(a, b):
    return lax.dot_general(a, b, (((1,), (1,)), ((), ())), preferred_element_type=F32)


def _dot_tn(a, b):
    return lax.dot_general(a, b, (((0,), (0,)), ((), ())), preferred_element_type=F32)


def _dot_exact_rhs01(a, m01):
    hi, mid, lo = _split3(a)
    return _dot(hi, m01) + _dot(mid, m01) + _dot(lo, m01)


def _dot_exact_lhs01(m01, a):
    hi, mid, lo = _split3(a)
    return _dot(m01, hi) + _dot(m01, mid) + _dot(m01, lo)


def _dot_hi(a, b):
    ah = a.astype(BF16)
    al = (a - ah.astype(F32)).astype(BF16)
    bh = b.astype(BF16)
    bl = (b - bh.astype(F32)).astype(BF16)
    return _dot(ah, bh) + _dot(ah, bl) + _dot(al, bh)


def _dot_rhs_hi(a, b):
    ah = a.astype(BF16)
    bh = b.astype(BF16)
    bl = (b - bh.astype(F32)).astype(BF16)
    return _dot(ah, bh) + _dot(ah, bl)


def _sigmoid(x):
    return 1.0 / (1.0 + jnp.exp(-x))


def _softplus(x):
    return jnp.maximum(x, 0.0) + jnp.log1p(jnp.exp(-jnp.abs(x)))


def _silu(x):
    return x * _sigmoid(x)


def _layer_norm(h, g, b):
    mu = jnp.mean(h, axis=-1, keepdims=True)
    d = h - mu
    var = jnp.mean(d * d, axis=-1, keepdims=True)
    return d * lax.rsqrt(var + LN_EPS) * g + b


def _proj_kernel(x_ref, *refs):
    n = len(refs) // 2
    xb = x_ref[...].astype(BF16)
    for w_ref, o_ref in zip(refs[:n], refs[n:]):
        o_ref[...] = _dot_nt(xb, w_ref[...])


def _project(x, wts, tm):
    m, k = x.shape
    return pl.pallas_call(
        _proj_kernel,
        grid=(m // tm,),
        in_specs=[pl.BlockSpec((tm, k), lambda i: (i, 0))]
                 + [pl.BlockSpec(w.shape, lambda i: (0, 0)) for w in wts],
        out_specs=[pl.BlockSpec((tm, w.shape[0]), lambda i: (i, 0)) for w in wts],
        out_shape=[jax.ShapeDtypeStruct((m, w.shape[0]), F32) for w in wts],
        compiler_params=_cparams(("parallel",)),
        name="proj",
    )(x, *wts)


def _gate_act(s, bias, alog, idx, n_f, n_g):
    z = s + bias
    logf = jnp.minimum(z, 0.0) - jnp.log1p(jnp.exp(-jnp.abs(z)))
    g = -jnp.exp(alog) * _softplus(z)
    beta = _sigmoid(z)
    return jnp.where(idx < n_f, logf, jnp.where(idx < n_f + n_g, g, beta))


def _small_kernel(x_ref, w_ref, wt_ref, bias_ref, alog_ref, biast_ref, alogt_ref, o_ref, ot_ref, *, n_f, n_g):
    xb = x_ref[...].astype(BF16)
    s = _dot_nt(xb, w_ref[...])
    st = _dot_nt(wt_ref[...], xb)
    lane = lax.broadcasted_iota(jnp.int32, s.shape, 1)
    o_ref[...] = _gate_act(s, bias_ref[...], alog_ref[...], lane, n_f, n_g)
    row = lax.broadcasted_iota(jnp.int32, st.shape, 0)
    ot_ref[...] = _gate_act(st, biast_ref[...], alogt_ref[...], row, n_f, n_g)


def _small_proj(x, wt_small, bias, alog, n_f, n_g, tm):
    m, k = x.shape
    nc = wt_small.shape[0]
    wt = wt_small.astype(BF16)
    w_pad = jnp.zeros((LANES, k), BF16).at[:nc].set(wt)
    bias_pad = jnp.zeros((1, LANES), F32).at[0, :nc].set(bias)
    alog_pad = jnp.zeros((1, LANES), F32).at[0, :nc].set(alog)
    kern = functools.partial(_small_kernel, n_f=n_f, n_g=n_g)
    full = lambda shape: pl.BlockSpec(shape, lambda i: (0,) * len(shape))
    return pl.pallas_call(
        kern,
        grid=(m // tm,),
        in_specs=[pl.BlockSpec((tm, k), lambda i: (i, 0)), full((LANES, k)), full((nc, k)),
                  full((1, LANES)), full((1, LANES)), full((nc, 1)), full((nc, 1))],
        out_specs=[pl.BlockSpec((tm, LANES), lambda i: (i, 0)), pl.BlockSpec((nc, tm), lambda i: (0, i))],
        out_shape=[jax.ShapeDtypeStruct((m, LANES), F32), jax.ShapeDtypeStruct((nc, m), F32)],
        compiler_params=_cparams(("parallel",)),
        name="gate_cols",
    )(x, w_pad, wt, bias_pad, alog_pad, bias.reshape(nc, 1), alog.reshape(nc, 1))


CUMSUM_CHUNK = 256


def _cumsum_kernel(gt_ref, ct_ref):
    n = gt_ref.shape[1] // CUMSUM_CHUNK
    r = lax.broadcasted_iota(jnp.int32, (CUMSUM_CHUNK, CUMSUM_CHUNK), 0)
    c = lax.broadcasted_iota(jnp.int32, (CUMSUM_CHUNK, CUMSUM_CHUNK), 1)
    triu = jnp.where(r <= c, 1.0, 0.0).astype(BF16)
    carry = jnp.zeros((gt_ref.shape[0], 1), F32)
    for i in range(n):
        sl = pl.ds(i * CUMSUM_CHUNK, CUMSUM_CHUNK)
        cs = _dot_exact_rhs01(gt_ref[:, sl], triu) + carry
        ct_ref[:, sl] = cs
        carry = cs[:, CUMSUM_CHUNK - 1:CUMSUM_CHUNK]


def _cumsum_rows(gt, n_seq, seq):
    nc = gt.shape[0]
    return pl.pallas_call(
        _cumsum_kernel,
        grid=(n_seq,),
        in_specs=[pl.BlockSpec((nc, seq), lambda b: (0, b))],
        out_specs=pl.BlockSpec((nc, seq), lambda b: (0, b)),
        out_shape=jax.ShapeDtypeStruct((nc, n_seq * seq), F32),
        compiler_params=_cparams(("parallel",)),
        name="logf_cumsum",
    )(gt)


def _fox_kernel(q_ref, k_ref, v_ref, ct_ref, o_ref, m_ref, l_ref, acc_ref, *, n_heads, head_dim):
    qi = pl.program_id(1)
    ki = pl.program_id(2)
    tq = q_ref.shape[0]
    tk = k_ref.shape[0]
    n_pairs = n_heads // 2
    scale = head_dim ** -0.5

    @pl.when(ki == 0)
    def _():
        m_ref[...] = jnp.full(m_ref.shape, NEG_BIG, F32)
        l_ref[...] = jnp.zeros(l_ref.shape, F32)
        acc_ref[...] = jnp.zeros(acc_ref.shape, F32)

    def step(masked):
        lane = lax.broadcasted_iota(jnp.int32, (tq, LANES), 1)
        low = lane < head_dim
        if masked:
            rq = lax.broadcasted_iota(jnp.int32, (tq, tk), 0)
            ck = lax.broadcasted_iota(jnp.int32, (tq, tk), 1)
            causal = ck <= rq
        for p in range(n_pairs):
            sl = pl.ds(p * LANES, LANES)
            qp = (q_ref[:, sl] * scale).astype(BF16)
            kp = k_ref[:, sl].astype(BF16)
            vp = v_ref[:, sl].astype(BF16)
            pv = []
            alphas = []
            for half in range(2):
                h = 2 * p + half
                qh = jnp.where(low if half == 0 else jnp.logical_not(low), qp, jnp.zeros_like(qp))
                s = _dot_nt(qh, kp) - ct_ref[h:h + 1, :]
                if masked:
                    s = jnp.where(causal, s, NEG_BIG)
                m_prev = m_ref[h]
                m_new = jnp.maximum(m_prev, jnp.max(s, axis=-1, keepdims=True))
                alpha = jnp.exp(m_prev - m_new)
                pr = jnp.exp(s - m_new[:, 0:1])
                l_ref[h] = alpha * l_ref[h] + jnp.sum(pr, axis=-1, keepdims=True)
                m_ref[h] = m_new
                pv.append(_dot(pr.astype(BF16), vp))
                alphas.append(alpha)
            a = jnp.where(low, alphas[0], alphas[1])
            acc_ref[p] = a * acc_ref[p] + jnp.where(low, pv[0], pv[1])

    @pl.when(ki < qi)
    def _():
        step(False)

    @pl.when(ki == qi)
    def _():
        step(True)
        lane = lax.broadcasted_iota(jnp.int32, (tq, LANES), 1)
        low = lane < head_dim
        for p in range(n_pairs):
            l = jnp.where(low, l_ref[2 * p], l_ref[2 * p + 1])
            o_ref[:, pl.ds(p * LANES, LANES)] = acc_ref[p] / l


def _fox_prompt(qkv, ct, n_rows_out, batch, seq, n_heads, head_dim, tq):
    w = n_heads * head_dim
    nq = seq // tq
    kern = functools.partial(_fox_kernel, n_heads=n_heads, head_dim=head_dim)
    return pl.pallas_call(
        kern,
        grid=(batch, nq, nq),
        in_specs=[pl.BlockSpec((tq, w), lambda b, i, j: (b * nq + i, 0)),
                  pl.BlockSpec((tq, w), lambda b, i, j: (b * nq + jnp.minimum(i, j), 1)),
                  pl.BlockSpec((tq, w), lambda b, i, j: (b * nq + jnp.minimum(i, j), 2)),
                  pl.BlockSpec((n_heads, tq), lambda b, i, j: (0, b * nq + jnp.minimum(i, j)))],
        out_specs=pl.BlockSpec((tq, w), lambda b, i, j: (b * nq + i, 0)),
        out_shape=jax.ShapeDtypeStruct((n_rows_out, w), F32),
        scratch_shapes=[pltpu.VMEM((n_heads, tq, LANES), F32), pltpu.VMEM((n_heads, tq, LANES), F32),
                        pltpu.VMEM((n_heads // 2, tq, LANES), F32)],
        compiler_params=_cparams(("parallel", "parallel", "arbitrary")),
        name="fox_prompt",
    )(qkv, qkv, qkv, ct)


def _pool_kernel(x_ref, halo_ref, w_ref, scale_ref, o_ref, buf_ref):
    i = pl.program_id(1)
    tb = x_ref.shape[0]
    hal = max(POOL_WINDOWS)
    buf_ref[0:hal, :] = jnp.where(i == 0, 0.0, halo_ref[...])
    buf_ref[hal:, :] = x_ref[...]
    pos = (i * tb + lax.broadcasted_iota(jnp.int32, (tb, 1), 0) + 1).astype(F32)
    for g, w in enumerate(POOL_WINDOWS):
        sl = pl.ds(g * LANES, LANES)
        win = buf_ref[hal:hal + tb, sl]
        for j in range(1, w):
            win = win + buf_ref[hal - j:hal - j + tb, sl]
        z = win / jnp.minimum(pos, float(w)) - x_ref[:, sl]
        o_ref[:, sl] = _dot(z.astype(BF16), w_ref[g]) * scale_ref[:, sl]


def _pool_prompt(ub, w_grp, scale, n_rows_out, batch, seq, tb):
    width = ub.shape[1]
    hal = max(POOL_WINDOWS)
    nb = seq // tb
    return pl.pallas_call(
        _pool_kernel,
        grid=(batch, nb),
        in_specs=[pl.BlockSpec((tb, width), lambda b, i: (b * nb + i, 0)),
                  pl.BlockSpec((hal, width), lambda b, i: (jnp.maximum((b * nb + i) * (tb // hal) - 1, 0), 0)),
                  pl.BlockSpec(w_grp.shape, lambda b, i: (0, 0, 0)),
                  pl.BlockSpec((1, width), lambda b, i: (0, 0))],
        out_specs=pl.BlockSpec((tb, width), lambda b, i: (b * nb + i, 0)),
        out_shape=jax.ShapeDtypeStruct((n_rows_out, width), F32),
        scratch_shapes=[pltpu.VMEM((hal + tb, width), F32)],
        compiler_params=_cparams(("parallel", "parallel")),
        name="pool_prompt",
    )(ub, ub, w_grp, scale)


def _gdn_kernel(x_ref, halo_ref, z_ref, gcol_ref, cw_ref, nw_ref, o_ref, s_out_ref, s_ref, buf_ref,
                *, n_heads, dk, dv, col_g, col_beta):
    i = pl.program_id(1)
    tb = x_ref.shape[0]
    c = GDN_CHUNK
    hal = SUBLANES

    @pl.when(i == 0)
    def _():
        s_ref[...] = jnp.zeros(s_ref.shape, F32)

    buf_ref[0:hal, :] = jnp.where(i == 0, 0.0, halo_ref[...])
    buf_ref[hal:, :] = x_ref[...]
    y = buf_ref[hal:hal + tb, :] * cw_ref[CONV_W - 1:CONV_W, :]
    for j in range(CONV_W - 1):
        off = hal - (CONV_W - 1) + j
        y = y + buf_ref[off:off + tb, :] * cw_ref[j:j + 1, :]
    qkv = _silu(y)

    rr = n_heads * c
    ri = lax.broadcasted_iota(jnp.int32, (rr, rr), 0)
    ci = lax.broadcasted_iota(jnp.int32, (rr, rr), 1)
    same = (ri // c) == (ci // c)
    incl = jnp.logical_and(same, ci <= ri)
    strict = jnp.logical_and(same, ci < ri)
    r1 = lax.broadcasted_iota(jnp.int32, (c, c), 0)
    c1 = lax.broadcasted_iota(jnp.int32, (c, c), 1)
    tril01 = jnp.where(c1 <= r1, 1.0, 0.0).astype(BF16)
    lane = lax.broadcasted_iota(jnp.int32, (rr, LANES), 1)
    n_steps = max(1, (c - 1).bit_length())

    def stack(parts):
        return jnp.concatenate(parts, axis=0)

    for ch in range(tb // c):
        rows = pl.ds(ch * c, c)
        lo_r, hi_r = ch * c, (ch + 1) * c
        gcol = gcol_ref[rows, :]
        decay_col = _dot_exact_lhs01(tril01, gcol)
        qs, ks, vs, betas, dcols, dlasts = [], [], [], [], [], []
        for h in range(n_heads):
            q = qkv[lo_r:hi_r, h * dk:(h + 1) * dk]
            k = qkv[lo_r:hi_r, (n_heads + h) * dk:(n_heads + h + 1) * dk]
            qs.append(q * lax.rsqrt(jnp.sum(q * q, axis=-1, keepdims=True) + L2_EPS) * dk ** -0.5)
            ks.append(k * lax.rsqrt(jnp.sum(k * k, axis=-1, keepdims=True) + L2_EPS))
            vs.append(qkv[lo_r:hi_r, 2 * n_heads * dk + h * dv:2 * n_heads * dk + (h + 1) * dv])
            betas.append(gcol[:, col_beta + h:col_beta + h + 1])
            dcols.append(decay_col[:, col_g + h:col_g + h + 1])
            dlasts.append(jnp.broadcast_to(decay_col[c - 1:c, col_g + h:col_g + h + 1], (c, 1)))
        q, k, v = stack(qs), stack(ks), stack(vs)
        beta, dcol, dlast = stack(betas), stack(dcols), stack(dlasts)

        d_hi, d_mid, d_lo = (t.astype(F32) for t in _split3(dcol))
        a_mat = jnp.where(lane == 0, d_hi, jnp.where(lane == 1, d_mid, jnp.where(lane == 2, d_lo,
                          jnp.where(lane < 6, 1.0, 0.0))))
        b_mat = jnp.where(lane == 3, -d_hi, jnp.where(lane == 4, -d_mid, jnp.where(lane == 5, -d_lo,
                          jnp.where(lane < 3, 1.0, 0.0))))
        gam = jnp.exp(jnp.where(incl, _dot_nt(a_mat.astype(BF16), b_mat.astype(BF16)), NEG_BIG))

        kb = k * beta
        kbf = k.astype(BF16)
        m = jnp.where(strict, _dot_nt(kb.astype(BF16), kbf) * gam, 0.0)
        qk = _dot_nt(q.astype(BF16), kbf) * gam
        ed = jnp.exp(dcol)
        x = jnp.concatenate([v * beta, kb * ed], axis=-1)
        p = -m
        for st in range(n_steps):
            x = x + _dot_rhs_hi(p, x)
            if st + 1 < n_steps:
                p = _dot_rhs_hi(p, p)
        value = x[:, :dv]
        kcum = x[:, dv:].astype(BF16)
        qd = (q * ed).astype(BF16)
        kd = (k * jnp.exp(dlast - dcol)).astype(BF16)
        vnews, o_state = [], []
        for h in range(n_heads):
            hr = slice(h * c, (h + 1) * c)
            s = s_ref[h]
            sb = s.astype(BF16)
            vnew = value[hr] - _dot(kcum[hr], sb)
            vnews.append(vnew)
            o_state.append(_dot(qd[hr], sb))
            s_ref[h] = s * jnp.exp(dlast[h * c:h * c + 1, :]) + _dot_tn(kd[hr], vnew.astype(BF16))
        o = stack(o_state) + _dot(qk.astype(BF16), stack(vnews).astype(BF16))
        o = o * lax.rsqrt(jnp.mean(o * o, axis=-1, keepdims=True) + RMS_EPS)
        for h in range(n_heads):
            zz = z_ref[rows, pl.ds(h * dv, dv)]
            o_ref[rows, pl.ds(h * dv, dv)] = o[h * c:(h + 1) * c] * nw_ref[...] * _silu(zz)

    @pl.when(i == pl.num_programs(1) - 1)
    def _():
        s_out_ref[...] = s_ref[...]


def _gdn_prompt(qkvc, zc, gcol, conv_w, onorm_w, n_rows_out, batch, seq, n_heads, dk, dv, col_g, col_beta, tb):
    nb = seq // tb
    wq = qkvc.shape[1]
    wz = zc.shape[1]
    kern = functools.partial(_gdn_kernel, n_heads=n_heads, dk=dk, dv=dv, col_g=col_g, col_beta=col_beta)
    return pl.pallas_call(
        kern,
        grid=(batch, nb),
        in_specs=[pl.BlockSpec((tb, wq), lambda b, i: (b * nb + i, 0)),
                  pl.BlockSpec((SUBLANES, wq), lambda b, i: (jnp.maximum((b * nb + i) * (tb // SUBLANES) - 1, 0), 0)),
                  pl.BlockSpec((tb, wz), lambda b, i: (b * nb + i, 0)),
                  pl.BlockSpec((tb, LANES), lambda b, i: (b * nb + i, 0)),
                  pl.BlockSpec((CONV_W, wq), lambda b, i: (0, 0)),
                  pl.BlockSpec((1, dv), lambda b, i: (0, 0))],
        out_specs=[pl.BlockSpec((tb, wz), lambda b, i: (b * nb + i, 0)),
                   pl.BlockSpec((None, n_heads, dk, dv), lambda b, i: (b, 0, 0, 0))],
        out_shape=[jax.ShapeDtypeStruct((n_rows_out, wz), F32),
                   jax.ShapeDtypeStruct((batch, n_heads, dk, dv), F32)],
        scratch_shapes=[pltpu.VMEM((n_heads, dk, dv), F32), pltpu.VMEM((SUBLANES + tb, wq), F32)],
        compiler_params=_cparams(("parallel", "arbitrary")),
        name="gdn_prompt",
    )(qkvc, qkvc, zc, gcol, conv_w, onorm_w)


def _fox_decode_kernel(pt_ref, q_ref, kn_ref, vn_ref, lfn_ref, *refs, layer, n_pages, head_dim):
    lf_refs = refs[:n_pages]
    k_hbm, v_hbm, o_ref, kbuf, vbuf, sem = refs[n_pages:]
    b = pl.program_id(0)
    slot = b % 2
    _, _, n_heads, _, page = kbuf.shape

    def start(seq, s):
        for j in range(n_pages):
            pg = pt_ref[seq * n_pages + j]
            pltpu.make_async_copy(k_hbm.at[layer, pg], kbuf.at[s, j], sem.at[0, s]).start()
            pltpu.make_async_copy(v_hbm.at[layer, pg], vbuf.at[s, j], sem.at[1, s]).start()

    @pl.when(b == 0)
    def _():
        start(0, 0)

    @pl.when(b + 1 < pl.num_programs(0))
    def _():
        start(b + 1, 1 - slot)

    pltpu.make_async_copy(k_hbm.at[layer, pl.ds(0, n_pages)], kbuf.at[slot], sem.at[0, slot]).wait()
    pltpu.make_async_copy(v_hbm.at[layer, pl.ds(0, n_pages)], vbuf.at[slot], sem.at[1, slot]).wait()

    qt = q_ref[...] * head_dim ** -0.5
    knt = kn_ref[...]
    vnt = vn_ref[...]
    lfn = lfn_ref[...]
    r = lax.broadcasted_iota(jnp.int32, (page, page), 0)
    c = lax.broadcasted_iota(jnp.int32, (page, page), 1)
    later01 = jnp.where(r > c, 1.0, 0.0).astype(BF16)

    later = jnp.zeros((n_heads, 1), F32)
    rps = [None] * n_pages
    for j in reversed(range(n_pages)):
        lf = lf_refs[j][...]
        rps[j] = _dot_exact_rhs01(lf, later01) + later
        later = later + jnp.sum(lf, axis=-1, keepdims=True)

    for h in range(n_heads):
        qcol = qt[:, h:h + 1]
        qb = jnp.broadcast_to(qcol, (head_dim, page))
        s_new = jnp.sum(qcol * knt[:, h:h + 1], axis=0, keepdims=True) - lfn[:, h:h + 1]
        m = s_new
        scores = []
        for j in range(n_pages):
            s = jnp.sum(kbuf[slot, j, h] * qb, axis=0, keepdims=True) + rps[j][h:h + 1, :]
            scores.append(s)
            m = jnp.maximum(m, jnp.max(s, axis=-1, keepdims=True))
        p_new = jnp.exp(s_new - m)
        denom = p_new
        acc = jnp.zeros((head_dim, page), F32)
        for j in range(n_pages):
            p = jnp.exp(scores[j] - m)
            denom = denom + jnp.sum(p, axis=-1, keepdims=True)
            acc = acc + vbuf[slot, j, h] * p
        o = jnp.sum(acc, axis=-1, keepdims=True) + p_new * vnt[:, h:h + 1]
        o_ref[:, h:h + 1] = o / denom


def _fox_decode(layer, page_table, qt, knt, vnt, lfn, cache_kt, cache_vt, lft):
    bd, n_pages = page_table.shape
    _, _, n_heads, head_dim, page = cache_kt.shape
    kern = functools.partial(_fox_decode_kernel, layer=layer, n_pages=n_pages, head_dim=head_dim)
    col = pl.BlockSpec((None, head_dim, n_heads), lambda b, pt: (b, 0, 0))

    def lf_spec(j):
        return pl.BlockSpec((None, None, n_heads, page), lambda b, pt: (layer, pt[b * n_pages + j], 0, 0))

    in_specs = ([col, col, col, pl.BlockSpec((None, 1, n_heads), lambda b, pt: (b, 0, 0))]
                + [lf_spec(j) for j in range(n_pages)]
                + [pl.BlockSpec(memory_space=pl.ANY), pl.BlockSpec(memory_space=pl.ANY)])
    return pl.pallas_call(
        kern,
        grid_spec=pltpu.PrefetchScalarGridSpec(
            num_scalar_prefetch=1, grid=(bd,), in_specs=in_specs,
            out_specs=pl.BlockSpec((None, head_dim, n_heads), lambda b, pt: (b, 0, 0)),
            scratch_shapes=[pltpu.VMEM((2, n_pages, n_heads, head_dim, page), F32),
                            pltpu.VMEM((2, n_pages, n_heads, head_dim, page), F32),
                            pltpu.SemaphoreType.DMA((2, 2))]),
        out_shape=jax.ShapeDtypeStruct((bd, head_dim, n_heads), F32),
        compiler_params=_cparams(("arbitrary",)),
        name="fox_decode",
    )(page_table.reshape(-1), qt, knt, vnt, lfn, *([lft] * n_pages), cache_kt, cache_vt)


def _pool_decode_kernel(st_ref, x_ref, w_ref, scale_ref, o_ref, new_ref, *, pos0):
    width = x_ref.shape[1]
    n_buf = st_ref.shape[1] // width
    for g, w in enumerate(POOL_WINDOWS):
        sl = pl.ds(g * LANES, LANES)
        win = x_ref[:, sl]
        for j in range(1, w):
            win = win + st_ref[:, pl.ds((n_buf - j) * width + g * LANES, LANES)]
        z = win / float(min(pos0 + 1, w)) - x_ref[:, sl]
        o_ref[:, sl] = _dot(z.astype(BF16), w_ref[g]) * scale_ref[:, sl]
    new_ref[:, 0:(n_buf - 1) * width] = st_ref[:, width:]
    new_ref[:, (n_buf - 1) * width:] = x_ref[...]


def _pool_decode(state_flat, x, w_grp, scale, pos0):
    bd, width = x.shape
    kern = functools.partial(_pool_decode_kernel, pos0=pos0)
    full = lambda a: pl.BlockSpec(a.shape, lambda i: (0,) * a.ndim)
    return pl.pallas_call(
        kern,
        grid=(1,),
        in_specs=[full(state_flat), full(x), full(w_grp), full(scale)],
        out_specs=[full(x), full(state_flat)],
        out_shape=[jax.ShapeDtypeStruct(x.shape, F32), jax.ShapeDtypeStruct(state_flat.shape, F32)],
        compiler_params=_cparams(("arbitrary",)),
        name="pool_decode",
    )(state_flat, x, w_grp, scale)


def _gdn_decode_kernel(x_ref, cs_ref, z_ref, g_ref, s_ref, cw_ref, nw_ref, o_ref, nc_ref, so_ref,
                       *, n_heads, dk, dv, col_g, col_beta):
    x = x_ref[...]
    cs = cs_ref[...]
    y = x * cw_ref[CONV_W - 1:CONV_W, :]
    for j in range(CONV_W - 1):
        y = y + cs[j:j + 1, :] * cw_ref[j:j + 1, :]
    qkv = _silu(y)
    nc_ref[0:CONV_W - 2, :] = cs[1:, :]
    nc_ref[CONV_W - 2:CONV_W - 1, :] = x
    gates = g_ref[...]
    for h in range(n_heads):
        q = qkv[:, h * dk:(h + 1) * dk]
        k = qkv[:, (n_heads + h) * dk:(n_heads + h + 1) * dk]
        v = qkv[:, 2 * n_heads * dk + h * dv:2 * n_heads * dk + (h + 1) * dv]
        q = q * lax.rsqrt(jnp.sum(q * q, axis=-1, keepdims=True) + L2_EPS) * dk ** -0.5
        k = k * lax.rsqrt(jnp.sum(k * k, axis=-1, keepdims=True) + L2_EPS)
        kc = jnp.transpose(jnp.broadcast_to(k, (dk, dk)))
        qc = jnp.transpose(jnp.broadcast_to(q, (dk, dk)))
        a = jnp.exp(gates[:, col_g + h:col_g + h + 1])
        beta = gates[:, col_beta + h:col_beta + h + 1]
        s = s_ref[h] * a
        u = (v - jnp.sum(kc * s, axis=0, keepdims=True)) * beta
        s = s + kc * u
        so_ref[h] = s
        o = jnp.sum(qc * s, axis=0, keepdims=True)
        o = o * lax.rsqrt(jnp.mean(o * o, axis=-1, keepdims=True) + RMS_EPS)
        o_ref[:, pl.ds(h * dv, dv)] = o * nw_ref[...] * _silu(z_ref[:, pl.ds(h * dv, dv)])


def _gdn_decode(x3, conv_state, z3, g3, s_state, conv_w, onorm_w, col_g, col_beta):
    bd, n_heads, dk, dv = s_state.shape
    wq = x3.shape[-1]
    wz = z3.shape[-1]
    kern = functools.partial(_gdn_decode_kernel, n_heads=n_heads, dk=dk, dv=dv, col_g=col_g, col_beta=col_beta)
    per_seq = lambda a: pl.BlockSpec((None,) + a.shape[1:], lambda b: (b,) + (0,) * (a.ndim - 1))
    full = lambda a: pl.BlockSpec(a.shape, lambda b: (0,) * a.ndim)
    return pl.pallas_call(
        kern,
        grid=(bd,),
        in_specs=[per_seq(x3), per_seq(conv_state), per_seq(z3), per_seq(g3), per_seq(s_state),
                  full(conv_w), full(onorm_w)],
        out_specs=[per_seq(z3), per_seq(conv_state), per_seq(s_state)],
        out_shape=[jax.ShapeDtypeStruct(z3.shape, F32), jax.ShapeDtypeStruct(conv_state.shape, F32),
                   jax.ShapeDtypeStruct(s_state.shape, F32)],
        compiler_params=_cparams(("parallel",)),
        name="gdn_decode",
    )(x3, conv_state, z3, g3, s_state, conv_w, onorm_w)


def _merge_kernel(oa_ref, ob_ref, oc_ref, gl_ref, x_ref, wb_ref, wo_ref, g_ref, b_ref, rw_ref, rb_ref,
                  x1_ref, gate_ref, idx_ref, rank_ref, counts_ref, cnt_ref, *, alpha, n_experts):
    d = x_ref.shape[1]
    m = None
    for bi, o_ref in enumerate((oa_ref, ob_ref, oc_ref)):
        proj = _dot(o_ref[...].astype(BF16), wb_ref[bi])
        term = _sigmoid(gl_ref[:, pl.ds(bi * d, d)]) * proj
        m = term if m is None else m + term
    h = alpha * x_ref[...] + _dot(m.astype(BF16), wo_ref[...])
    x1 = _layer_norm(h, g_ref[...], b_ref[...])
    x1_ref[...] = x1
    logits = _dot_hi(x1, rw_ref[...]) + rb_ref[...]
    lane = lax.broadcasted_iota(jnp.int32, logits.shape, 1)
    lg = jnp.where(lane < n_experts, logits, NEG_BIG)
    vals = []
    picks = []
    gate = jnp.zeros(logits.shape, F32)
    idx = jnp.zeros(logits.shape, jnp.int32)
    for k in range(TOP_K):
        mx = jnp.max(lg, axis=-1, keepdims=True)
        ix = jnp.min(jnp.where(lg == mx, lane, LANES), axis=-1, keepdims=True)
        vals.append(mx)
        picks.append(lane == ix)
        idx = jnp.where(lane == k, ix, idx)
        lg = jnp.where(lane == ix, NEG_BIG, lg)
    exps = [jnp.exp(v - vals[0]) for v in vals]
    tot = exps[0]
    for e in exps[1:]:
        tot = tot + e
    for k in range(TOP_K):
        gate = jnp.where(lane == k, exps[k] / tot, gate)
    gate_ref[...] = gate
    idx_ref[...] = idx

    @pl.when(pl.program_id(0) == 0)
    def _():
        cnt_ref[...] = jnp.zeros(cnt_ref.shape, F32)

    tm = logits.shape[0]
    chosen = jnp.zeros(logits.shape, F32)
    for pk in picks:
        chosen = jnp.where(pk, 1.0, chosen)
    r = lax.broadcasted_iota(jnp.int32, (tm, tm), 0)
    c = lax.broadcasted_iota(jnp.int32, (tm, tm), 1)
    before = _dot(jnp.where(c < r, 1.0, 0.0).astype(BF16), chosen.astype(BF16)) + cnt_ref[...]
    rank = jnp.zeros(logits.shape, F32)
    for k, pk in enumerate(picks):
        rank = jnp.where(lane == k, jnp.sum(jnp.where(pk, before, 0.0), axis=-1, keepdims=True), rank)
    rank_ref[...] = rank.astype(jnp.int32)
    cnt_ref[...] = cnt_ref[...] + jnp.sum(chosen, axis=0, keepdims=True)
    counts_ref[...] = cnt_ref[...]


def _merge(oa, ob, oc, gl, x, wb, wo, ln_g, ln_b, rw, rb, alpha, n_experts, tm):
    nt, d = x.shape
    w = oa.shape[1]
    kern = functools.partial(_merge_kernel, alpha=alpha, n_experts=n_experts)
    rows = lambda width: pl.BlockSpec((tm, width), lambda i: (i, 0))
    full = lambda a: pl.BlockSpec(a.shape, lambda i: (0,) * a.ndim)
    return pl.pallas_call(
        kern,
        grid=(nt // tm,),
        in_specs=[rows(w), rows(w), rows(w), rows(gl.shape[1]), rows(d), full(wb), full(wo), full(ln_g),
                  full(ln_b), full(rw), full(rb)],
        out_specs=[rows(d), rows(LANES), rows(LANES), rows(LANES), pl.BlockSpec((1, LANES), lambda i: (0, 0))],
        out_shape=[jax.ShapeDtypeStruct((nt, d), F32), jax.ShapeDtypeStruct((nt, LANES), F32),
                   jax.ShapeDtypeStruct((nt, LANES), jnp.int32), jax.ShapeDtypeStruct((nt, LANES), jnp.int32),
                   jax.ShapeDtypeStruct((1, LANES), F32)],
        scratch_shapes=[pltpu.VMEM((1, LANES), F32)],
        compiler_params=_cparams(("arbitrary",)),
        name="merge_ln_router",
    )(oa, ob, oc, gl, x, wb, wo, ln_g, ln_b, rw, rb)


def _moe_kernel(blk_e_ref, tok0_ref, tok_next_ref, dst_prev_ref, x_hbm, wu_ref, bu_ref, wd_ref, bd_ref,
                out_hbm, xs0_ref, xs1_ref, ys0_ref, ys1_ref, gsem, ssem):
    i = pl.program_id(0)
    blk = xs0_ref.shape[0]
    dff = wd_ref.shape[0]
    xs = (xs0_ref, xs1_ref)
    ys = (ys0_ref, ys1_ref)

    def gather(idx_ref, s):
        for r in range(blk):
            pltpu.make_async_copy(x_hbm.at[pl.ds(idx_ref[0, r], 1), :], xs[s].at[pl.ds(r, 1), :],
                                  gsem.at[s]).start()

    def wait_gather(s):
        pltpu.make_async_copy(x_hbm.at[pl.ds(0, blk), :], xs[s], gsem.at[s]).wait()

    def wait_scatter(s):
        pltpu.make_async_copy(ys[s], out_hbm.at[pl.ds(0, blk), :], ssem.at[s]).wait()

    @pl.when(i == 0)
    def _():
        ys0_ref[...] = jnp.zeros(ys0_ref.shape, F32)
        ys1_ref[...] = jnp.zeros(ys1_ref.shape, F32)
        gather(tok0_ref, 0)

    def step(slot):
        other = 1 - slot
        wait_gather(slot)

        @pl.when(i >= 1)
        def _():
            wait_scatter(slot)

        xb = xs[slot][...].astype(BF16)
        hb = _dot(xb, wu_ref[...]) + bu_ref[...]
        gt = jnp.minimum(hb[:, :dff], SWIGLU_LIMIT)
        up = jnp.clip(hb[:, dff:], -SWIGLU_LIMIT, SWIGLU_LIMIT)
        act = (up + 1.0) * (gt * _sigmoid(SWIGLU_ALPHA * gt))
        ys[slot][...] = _dot(act.astype(BF16), wd_ref[...]) + bd_ref[...]

        gather(tok_next_ref, other)
        for r in range(blk):
            pltpu.make_async_copy(ys[other].at[pl.ds(r, 1), :], out_hbm.at[pl.ds(dst_prev_ref[0, r], 1), :],
                                  ssem.at[other]).start()

        @pl.when(i == pl.num_programs(0) - 1)
        def _():
            wait_gather(other)
            wait_scatter(other)

    @pl.when(i % 2 == 0)
    def _():
        step(0)

    @pl.when(i % 2 == 1)
    def _():
        step(1)


def _moe(x1, blk_e, row_tok, row_dst, wu, bu, wd, bd, n_out_rows):
    nt, d = x1.shape
    n_blk = row_tok.shape[0]
    blk = row_tok.shape[-1]
    n_exp, _, dff2 = wu.shape
    idx_spec = lambda f: pl.BlockSpec((None, 1, blk), f, memory_space=pltpu.SMEM)
    return pl.pallas_call(
        _moe_kernel,
        grid_spec=pltpu.PrefetchScalarGridSpec(
            num_scalar_prefetch=1, grid=(n_blk + 1,),
            in_specs=[idx_spec(lambda i, be: (0, 0, 0)),
                      idx_spec(lambda i, be: (jnp.minimum(i + 1, n_blk - 1), 0, 0)),
                      idx_spec(lambda i, be: (i, 0, 0)),
                      pl.BlockSpec(memory_space=pl.ANY),
                      pl.BlockSpec((None, d, dff2), lambda i, be: (be[i], 0, 0)),
                      pl.BlockSpec((None, 1, dff2), lambda i, be: (be[i], 0, 0)),
                      pl.BlockSpec((None, dff2 // 2, d), lambda i, be: (be[i], 0, 0)),
                      pl.BlockSpec((None, 1, d), lambda i, be: (be[i], 0, 0))],
            out_specs=pl.BlockSpec(memory_space=pl.ANY),
            scratch_shapes=[pltpu.VMEM((blk, d), F32), pltpu.VMEM((blk, d), F32),
                            pltpu.VMEM((blk, d), F32), pltpu.VMEM((blk, d), F32),
                            pltpu.SemaphoreType.DMA((2,)), pltpu.SemaphoreType.DMA((2,))]),
        out_shape=jax.ShapeDtypeStruct((n_out_rows, d), F32),
        compiler_params=_cparams(("arbitrary",)),
        name="moe_experts",
    )(blk_e, row_tok, row_tok, row_dst, x1, wu, bu, wd, bd)


def _route(idx, rank, counts, n_experts, nt):
    tk = nt * TOP_K
    pcounts = (counts + MOE_BLOCK - 1) // MOE_BLOCK * MOE_BLOCK
    pend = jnp.cumsum(pcounts)
    pstart = pend - pcounts
    experts = jnp.arange(n_experts, dtype=jnp.int32)
    dest = rank + jnp.sum(jnp.where(idx[:, :, None] == experts, pstart, 0), axis=-1)
    n_rows = -(-(tk + n_experts * (MOE_BLOCK - 1)) // MOE_BLOCK) * MOE_BLOCK
    n_blk = n_rows // MOE_BLOCK
    row_pair = jnp.full((n_rows,), -1, jnp.int32).at[dest.reshape(-1)].set(jnp.arange(tk, dtype=jnp.int32))
    dump = TOP_K * nt + jnp.arange(n_rows, dtype=jnp.int32) % MOE_BLOCK
    row_tok = jnp.where(row_pair >= 0, row_pair // TOP_K, 0)
    row_dst = jnp.where(row_pair >= 0, (row_pair % TOP_K) * nt + row_pair // TOP_K, dump)
    row_dst = jnp.concatenate([dump[:MOE_BLOCK], row_dst])
    blk_start = jnp.arange(n_blk, dtype=jnp.int32) * MOE_BLOCK
    blk_e = jnp.minimum(jnp.sum((pend[None, :] <= blk_start[:, None]).astype(jnp.int32), axis=1), n_experts - 1)
    blk_e = jnp.concatenate([blk_e, blk_e[-1:]])
    return blk_e, row_tok.reshape(n_blk, 1, MOE_BLOCK), row_dst.reshape(n_blk + 1, 1, MOE_BLOCK)


def _ple_kernel(x1_ref, y0_ref, y1_ref, y2_ref, y3_ref, gate_ref, p_ref, wg_ref, wp_ref, g_ref, b_ref, o_ref, *, alpha):
    gate = gate_ref[...]
    f = None
    for k, y_ref in enumerate((y0_ref, y1_ref, y2_ref, y3_ref)):
        term = gate[:, k:k + 1] * y_ref[...]
        f = term if f is None else f + term
    h = alpha * x1_ref[...] + f
    gpe = _sigmoid(_dot(h.astype(BF16), wg_ref[...])) * _dot(p_ref[...].astype(BF16), wp_ref[...])
    o_ref[...] = _layer_norm(h + gpe, g_ref[...], b_ref[...])


def _ple(x1, out4, gate, p, wg, wp, ln_g, ln_b, alpha, tm):
    nt, d = x1.shape
    nb = nt // tm
    kern = functools.partial(_ple_kernel, alpha=alpha)
    rows = lambda width: pl.BlockSpec((tm, width), lambda i: (i, 0))
    full = lambda a: pl.BlockSpec(a.shape, lambda i: (0,) * a.ndim)
    slot = lambda k: pl.BlockSpec((tm, d), lambda i: (k * nb + i, 0))
    return pl.pallas_call(
        kern,
        grid=(nb,),
        in_specs=[rows(d), slot(0), slot(1), slot(2), slot(3), rows(LANES), rows(p.shape[1]), full(wg), full(wp),
                  full(ln_g), full(ln_b)],
        out_specs=rows(d),
        out_shape=jax.ShapeDtypeStruct((nt, d), F32),
        compiler_params=_cparams(("parallel",)),
        name="combine_ple_ln",
    )(x1, out4, out4, out4, out4, gate, p, wg, wp, ln_g, ln_b)


def _row_tile(n, cap):
    best = None
    for t in range(LANES, cap + 1, LANES):
        if n % t == 0:
            best = t
    assert best is not None, n
    return best


def kernel(x_prompt, x_sample, cache_k, cache_v, cache_logf, state_pool, state_conv, state_delta, page_table,
           p_prompt, p_sample, w_in, b_f, w_grp, pool_scale, conv_w, a_log, dt_bias, onorm_w, w_branch, w_o,
           ln1_g, ln1_b, router_w, router_b, w_up, b_up, w_down, b_down, w_pe, w_peg, ln2_g, ln2_b):
    batch, seq, d = x_prompt.shape
    bd, dec_seq, _ = x_sample.shape
    assert dec_seq == 1
    depth = w_in.shape[0]
    _, _, page, n_heads, head_dim = cache_k.shape
    width = n_heads * head_dim
    _, _, gh, dk, dv = state_delta.shape
    gqkv = state_conv.shape[3]
    n_branch = w_branch.shape[1]
    n_exp = router_w.shape[2]
    ple_dim = p_prompt.shape[3]
    n_buf = state_pool.shape[2]
    past = page_table.shape[1] * page
    alpha = (2 * depth) ** 0.25
    n_p = batch * seq
    nt = n_p + bd
    tm = _row_tile(nt, 512)

    sizes = (width, width, width, n_heads, width, gqkv, gh, gh, width, n_branch * d)
    offs = [0]
    for s in sizes:
        offs.append(offs[-1] + s)
    o_q, _, _, o_f, o_ub, o_c, o_a, o_b, o_z, o_gl, o_end = offs
    assert o_end == w_in.shape[2]
    col_g, col_beta = n_heads, n_heads + gh

    x = jnp.concatenate([x_prompt.reshape(n_p, d), x_sample.reshape(bd, d)], axis=0)
    lft = jnp.swapaxes(cache_logf, 2, 3)
    cache_kt = jnp.transpose(cache_k, (0, 1, 3, 4, 2))
    cache_vt = jnp.transpose(cache_v, (0, 1, 3, 4, 2))
    rows_p, rows_s = [], []
    for l in range(depth):
        wt = jnp.swapaxes(w_in[l], 0, 1)
        rows = lambda a, n: wt[a:a + n].astype(BF16)
        wt_small = jnp.concatenate([wt[o_f:o_f + n_heads], wt[o_a:o_a + gh], wt[o_b:o_b + gh]], axis=0)
        bias = jnp.concatenate([b_f[l], dt_bias[l], jnp.zeros((gh,), F32)])
        alog = jnp.concatenate([jnp.zeros((n_heads,), F32), a_log[l], jnp.zeros((gh,), F32)])
        qkv, ub, qkvc, zc = _project(x, [rows(o_q, 3 * width), rows(o_ub, width), rows(o_c, gqkv),
                                         rows(o_z, width)], tm)
        gl, = _project(x, [rows(o_gl, n_branch * d)], tm)
        gcol, gt = _small_proj(x, wt_small, bias, alog, n_heads, gh, tm)
        ct = _cumsum_rows(gt, batch, seq)

        wg_bf = w_grp[l].astype(BF16)
        pscale = pool_scale[l].reshape(1, width)
        onw = onorm_w[l].reshape(1, dv)
        oa = _fox_prompt(qkv, ct, n_p, batch, seq, n_heads, head_dim, min(seq, 512))
        ob = _pool_prompt(ub, wg_bf, pscale, n_p, batch, seq, min(seq, 256))
        oc, s_p = _gdn_prompt(qkvc, zc, gcol, conv_w[l], onw, n_p, batch, seq, gh, dk, dv, col_g, col_beta,
                              min(seq, 2 * GDN_CHUNK))

        qs = qkv[n_p:]
        q3 = qs[:, :width].reshape(bd, n_heads, head_dim)
        kn3 = qs[:, width:2 * width].reshape(bd, n_heads, head_dim)
        vn3 = qs[:, 2 * width:].reshape(bd, n_heads, head_dim)
        lfn = gcol[n_p:, :n_heads].reshape(bd, 1, n_heads)
        to_cols = lambda a: jnp.swapaxes(a, 1, 2)
        oa_s = to_cols(_fox_decode(l, page_table, to_cols(q3), to_cols(kn3), to_cols(vn3), lfn,
                                   cache_kt, cache_vt, lft)).reshape(bd, width)
        ob_s, pool_new = _pool_decode(state_pool[l].reshape(bd, n_buf * width), ub[n_p:], wg_bf, pscale, past)
        oc_s, conv_new, s_new = _gdn_decode(qkvc[n_p:].reshape(bd, 1, gqkv), state_conv[l],
                                            zc[n_p:].reshape(bd, 1, width), gcol[n_p:].reshape(bd, 1, LANES),
                                            state_delta[l], conv_w[l], onw, col_g, col_beta)
        oa = jnp.concatenate([oa, oa_s], axis=0)
        ob = jnp.concatenate([ob, ob_s], axis=0)
        oc = jnp.concatenate([oc, oc_s.reshape(bd, width)], axis=0)

        rw = jnp.zeros((d, LANES), F32).at[:, :n_exp].set(router_w[l])
        rb = jnp.zeros((1, LANES), F32).at[0, :n_exp].set(router_b[l])
        x1, gate, idx, rank, counts = _merge(oa, ob, oc, gl, x, w_branch[l].astype(BF16), w_o[l].astype(BF16),
                                             ln1_g[l].reshape(1, d), ln1_b[l].reshape(1, d), rw, rb, alpha, n_exp, tm)
        blk_e, row_tok, row_dst = _route(idx[:, :TOP_K], rank[:, :TOP_K], counts[0, :n_exp].astype(jnp.int32),
                                         n_exp, nt)
        out4 = _moe(x1, blk_e, row_tok, row_dst, w_up[l].astype(BF16), b_up[l].reshape(n_exp, 1, -1),
                    w_down[l].astype(BF16), b_down[l].reshape(n_exp, 1, -1), TOP_K * nt + MOE_BLOCK)
        p = jnp.concatenate([p_prompt[l].reshape(n_p, ple_dim), p_sample[l].reshape(bd, ple_dim)], axis=0)
        x = _ple(x1, out4, gate, p, w_peg[l].astype(BF16), w_pe[l].astype(BF16), ln2_g[l].reshape(1, d),
                 ln2_b[l].reshape(1, d), alpha, tm)

        rows_p.append((qkv[:n_p, width:2 * width].reshape(batch, seq, n_heads, head_dim),
                       qkv[:n_p, 2 * width:].reshape(batch, seq, n_heads, head_dim),
                       gcol[:n_p, :n_heads].reshape(batch, seq, n_heads),
                       ub[:n_p].reshape(batch, seq, width)[:, seq - n_buf:],
                       qkvc[:n_p].reshape(batch, seq, gqkv)[:, seq - (CONV_W - 1):],
                       s_p))
        rows_s.append((kn3.reshape(bd, 1, n_heads, head_dim), vn3.reshape(bd, 1, n_heads, head_dim),
                       gcol[n_p:, :n_heads].reshape(bd, 1, n_heads), pool_new.reshape(bd, n_buf, width),
                       conv_new, s_new))

    def stk(rows, i):
        return jnp.stack([r[i] for r in rows])

    return (x[:n_p].reshape(batch, seq, d), x[n_p:].reshape(bd, 1, d),
            stk(rows_p, 0), stk(rows_p, 1), stk(rows_p, 2), stk(rows_p, 3), stk(rows_p, 4), stk(rows_p, 5),
            stk(rows_s, 0), stk(rows_s, 1), stk(rows_s, 2), stk(rows_s, 3), stk(rows_s, 4), stk(rows_s, 5))
```

```python
import functools

import jax
import jax.numpy as jnp
from jax import lax
from jax.experimental import pallas as pl
from jax.experimental.pallas import tpu as pltpu

F32 = jnp.float32
BF16 = jnp.bfloat16

POOL_WINDOWS = (2, 4, 8, 16)
CONV_W = 4
GDN_CHUNK = 64
TOP_K = 4
SWIGLU_LIMIT = 7.0
SWIGLU_ALPHA = 1.702
LN_EPS = 1e-5
RMS_EPS = 1e-6
L2_EPS = 1e-6
MOE_BLOCK = 128

LANES = 128
SUBLANES = 8
VMEM_LIMIT = 56 * 1024 * 1024

NEG_BIG = -1e30


def _cparams(sem):
    return pltpu.CompilerParams(dimension_semantics=sem, vmem_limit_bytes=VMEM_LIMIT)


def _split3(a):
    hi = a.astype(BF16)
    r1 = a - hi.astype(F32)
    mid = r1.astype(BF16)
    lo = (r1 - mid.astype(F32)).astype(BF16)
    return hi, mid, lo


def _dot(a, b):
    return jnp.dot(a, b, preferred_element_type=F32)


def _dot_nt(a, b):
    return lax.dot_general(a, b, (((1,), (1,)), ((), ())), preferred_element_type=F32)


def _dot_tn(a, b):
    return lax.dot_general(a, b, (((0,), (0,)), ((), ())), preferred_element_type=F32)


def _dot_exact_rhs01(a, m01):
    hi, mid, lo = _split3(a)
    return _dot(hi, m01) + _dot(mid, m01) + _dot(lo, m01)


def _dot_exact_lhs01(m01, a):
    hi, mid, lo = _split3(a)
    return _dot(m01, hi) + _dot(m01, mid) + _dot(m01, lo)


def _dot_hi(a, b):
    ah = a.astype(BF16)
    al = (a - ah.astype(F32)).astype(BF16)
    bh = b.astype(BF16)
    bl = (b - bh.astype(F32)).astype(BF16)
    return _dot(ah, bh) + _dot(ah, bl) + _dot(al, bh)


def _dot_rhs_hi(a, b):
    ah = a.astype(BF16)
    bh = b.astype(BF16)
    bl = (b - bh.astype(F32)).astype(BF16)
    return _dot(ah, bh) + _dot(ah, bl)


def _sigmoid(x):
    return 1.0 / (1.0 + jnp.exp(-x))


def _softplus(x):
    return jnp.maximum(x, 0.0) + jnp.log1p(jnp.exp(-jnp.abs(x)))


def _silu(x):
    return x * _sigmoid(x)


def _tiles_to_rows(ref, n, ds):
    return jnp.concatenate([ref[pl.ds(s, n, stride=ds), :] for s in range(ds)], axis=-1)


def _rows_to_tiles(ref, val, n, ds):
    for s in range(ds):
        ref[pl.ds(s, n, stride=ds), :] = val[:, s * LANES:(s + 1) * LANES]


def _layer_norm(h, g, b):
    mu = jnp.mean(h, axis=-1, keepdims=True)
    d = h - mu
    var = jnp.mean(d * d, axis=-1, keepdims=True)
    return d * lax.rsqrt(var + LN_EPS) * g + b


def _proj_kernel(x_ref, *refs):
    n = len(refs) // 2
    xb = x_ref[...].astype(BF16)
    for w_ref, o_ref in zip(refs[:n], refs[n:]):
        o_ref[...] = _dot_nt(xb, w_ref[...])


def _project(x, wts, tm):
    m, k = x.shape
    return pl.pallas_call(
        _proj_kernel,
        grid=(m // tm,),
        in_specs=[pl.BlockSpec((tm, k), lambda i: (i, 0))]
                 + [pl.BlockSpec(w.shape, lambda i: (0, 0)) for w in wts],
        out_specs=[pl.BlockSpec((tm, w.shape[0]), lambda i: (i, 0)) for w in wts],
        out_shape=[jax.ShapeDtypeStruct((m, w.shape[0]), F32) for w in wts],
        compiler_params=_cparams(("parallel",)),
        name="proj",
    )(x, *wts)


def _gate_act(s, bias, alog, idx, n_f, n_g):
    z = s + bias
    logf = jnp.minimum(z, 0.0) - jnp.log1p(jnp.exp(-jnp.abs(z)))
    g = -jnp.exp(alog) * _softplus(z)
    beta = _sigmoid(z)
    return jnp.where(idx < n_f, logf, jnp.where(idx < n_f + n_g, g, beta))


def _small_kernel(x_ref, w_ref, wt_ref, bias_ref, alog_ref, biast_ref, alogt_ref, o_ref, ot_ref, *, n_f, n_g):
    xb = x_ref[...].astype(BF16)
    s = _dot_nt(xb, w_ref[...])
    st = _dot_nt(wt_ref[...], xb)
    lane = lax.broadcasted_iota(jnp.int32, s.shape, 1)
    o_ref[...] = _gate_act(s, bias_ref[...], alog_ref[...], lane, n_f, n_g)
    row = lax.broadcasted_iota(jnp.int32, st.shape, 0)
    ot_ref[...] = _gate_act(st, biast_ref[...], alogt_ref[...], row, n_f, n_g)


def _small_proj(x, wt_small, bias, alog, n_f, n_g, tm):
    m, k = x.shape
    nc = wt_small.shape[0]
    wt = wt_small.astype(BF16)
    w_pad = jnp.zeros((LANES, k), BF16).at[:nc].set(wt)
    bias_pad = jnp.zeros((1, LANES), F32).at[0, :nc].set(bias)
    alog_pad = jnp.zeros((1, LANES), F32).at[0, :nc].set(alog)
    kern = functools.partial(_small_kernel, n_f=n_f, n_g=n_g)
    full = lambda shape: pl.BlockSpec(shape, lambda i: (0,) * len(shape))
    return pl.pallas_call(
        kern,
        grid=(m // tm,),
        in_specs=[pl.BlockSpec((tm, k), lambda i: (i, 0)), full((LANES, k)), full((nc, k)),
                  full((1, LANES)), full((1, LANES)), full((nc, 1)), full((nc, 1))],
        out_specs=[pl.BlockSpec((tm, LANES), lambda i: (i, 0)), pl.BlockSpec((nc, tm), lambda i: (0, i))],
        out_shape=[jax.ShapeDtypeStruct((m, LANES), F32), jax.ShapeDtypeStruct((nc, m), F32)],
        compiler_params=_cparams(("parallel",)),
        name="gate_cols",
    )(x, w_pad, wt, bias_pad, alog_pad, bias.reshape(nc, 1), alog.reshape(nc, 1))


CUMSUM_CHUNK = 256


def _cumsum_kernel(gt_ref, ct_ref):
    n = gt_ref.shape[1] // CUMSUM_CHUNK
    r = lax.broadcasted_iota(jnp.int32, (CUMSUM_CHUNK, CUMSUM_CHUNK), 0)
    c = lax.broadcasted_iota(jnp.int32, (CUMSUM_CHUNK, CUMSUM_CHUNK), 1)
    triu = jnp.where(r <= c, 1.0, 0.0).astype(BF16)
    carry = jnp.zeros((gt_ref.shape[0], 1), F32)
    for i in range(n):
        sl = pl.ds(i * CUMSUM_CHUNK, CUMSUM_CHUNK)
        cs = _dot_exact_rhs01(gt_ref[:, sl], triu) + carry
        ct_ref[:, sl] = cs
        carry = cs[:, CUMSUM_CHUNK - 1:CUMSUM_CHUNK]


def _cumsum_rows(gt, n_seq, seq):
    nc = gt.shape[0]
    return pl.pallas_call(
        _cumsum_kernel,
        grid=(n_seq,),
        in_specs=[pl.BlockSpec((nc, seq), lambda b: (0, b))],
        out_specs=pl.BlockSpec((nc, seq), lambda b: (0, b)),
        out_shape=jax.ShapeDtypeStruct((nc, n_seq * seq), F32),
        compiler_params=_cparams(("parallel",)),
        name="logf_cumsum",
    )(gt)


def _fox_kernel(q_ref, k_ref, v_ref, ct_ref, o_ref, m_ref, l_ref, acc_ref, *, n_heads, head_dim):
    qi = pl.program_id(1)
    ki = pl.program_id(2)
    tq = q_ref.shape[0]
    tk = k_ref.shape[0]
    n_pairs = n_heads // 2
    scale = head_dim ** -0.5

    @pl.when(ki == 0)
    def _():
        m_ref[...] = jnp.full(m_ref.shape, NEG_BIG, F32)
        l_ref[...] = jnp.zeros(l_ref.shape, F32)
        acc_ref[...] = jnp.zeros(acc_ref.shape, F32)

    def step(masked):
        lane = lax.broadcasted_iota(jnp.int32, (tq, LANES), 1)
        low = lane < head_dim
        if masked:
            rq = lax.broadcasted_iota(jnp.int32, (tq, tk), 0)
            ck = lax.broadcasted_iota(jnp.int32, (tq, tk), 1)
            causal = ck <= rq
        for p in range(n_pairs):
            sl = pl.ds(p * LANES, LANES)
            qp = (q_ref[:, sl] * scale).astype(BF16)
            kp = k_ref[:, sl].astype(BF16)
            vp = v_ref[:, sl].astype(BF16)
            pv = []
            alphas = []
            for half in range(2):
                h = 2 * p + half
                qh = jnp.where(low if half == 0 else jnp.logical_not(low), qp, jnp.zeros_like(qp))
                s = _dot_nt(qh, kp) - ct_ref[h:h + 1, :]
                if masked:
                    s = jnp.where(causal, s, NEG_BIG)
                m_prev = m_ref[h]
                m_new = jnp.maximum(m_prev, jnp.max(s, axis=-1, keepdims=True))
                alpha = jnp.exp(m_prev - m_new)
                pr = jnp.exp(s - m_new[:, 0:1])
                l_ref[h] = alpha * l_ref[h] + jnp.sum(pr, axis=-1, keepdims=True)
                m_ref[h] = m_new
                pv.append(_dot(pr.astype(BF16), vp))
                alphas.append(alpha)
            a = jnp.where(low, alphas[0], alphas[1])
            acc_ref[p] = a * acc_ref[p] + jnp.where(low, pv[0], pv[1])

    @pl.when(ki < qi)
    def _():
        step(False)

    @pl.when(ki == qi)
    def _():
        step(True)
        lane = lax.broadcasted_iota(jnp.int32, (tq, LANES), 1)
        low = lane < head_dim
        for p in range(n_pairs):
            l = jnp.where(low, l_ref[2 * p], l_ref[2 * p + 1])
            o_ref[:, pl.ds(p * LANES, LANES)] = acc_ref[p] / l


def _fox_prompt(q, k, v, ct, n_rows_out, batch, seq, n_heads, head_dim, tq):
    w = n_heads * head_dim
    nq = seq // tq
    kern = functools.partial(_fox_kernel, n_heads=n_heads, head_dim=head_dim)
    return pl.pallas_call(
        kern,
        grid=(batch, nq, nq),
        in_specs=[pl.BlockSpec((tq, w), lambda b, i, j: (b * nq + i, 0)),
                  pl.BlockSpec((tq, w), lambda b, i, j: (b * nq + jnp.minimum(i, j), 0)),
                  pl.BlockSpec((tq, w), lambda b, i, j: (b * nq + jnp.minimum(i, j), 0)),
                  pl.BlockSpec((n_heads, tq), lambda b, i, j: (0, b * nq + jnp.minimum(i, j)))],
        out_specs=pl.BlockSpec((tq, w), lambda b, i, j: (b * nq + i, 0)),
        out_shape=jax.ShapeDtypeStruct((n_rows_out, w), F32),
        scratch_shapes=[pltpu.VMEM((n_heads, tq, LANES), F32), pltpu.VMEM((n_heads, tq, LANES), F32),
                        pltpu.VMEM((n_heads // 2, tq, LANES), F32)],
        compiler_params=_cparams(("parallel", "parallel", "arbitrary")),
        name="fox_prompt",
    )(q, k, v, ct)


def _pool_kernel(x_ref, halo_ref, w_ref, scale_ref, o_ref, buf_ref):
    i = pl.program_id(1)
    tb = x_ref.shape[0]
    hal = max(POOL_WINDOWS)
    buf_ref[0:hal, :] = jnp.where(i == 0, 0.0, halo_ref[...])
    buf_ref[hal:, :] = x_ref[...]
    pos = (i * tb + lax.broadcasted_iota(jnp.int32, (tb, 1), 0) + 1).astype(F32)
    for g, w in enumerate(POOL_WINDOWS):
        sl = pl.ds(g * LANES, LANES)
        win = buf_ref[hal:hal + tb, sl]
        for j in range(1, w):
            win = win + buf_ref[hal - j:hal - j + tb, sl]
        z = win / jnp.minimum(pos, float(w)) - x_ref[:, sl]
        o_ref[:, sl] = _dot(z.astype(BF16), w_ref[g]) * scale_ref[:, sl]


def _pool_prompt(ub, w_grp, scale, n_rows_out, batch, seq, tb):
    width = ub.shape[1]
    hal = max(POOL_WINDOWS)
    nb = seq // tb
    return pl.pallas_call(
        _pool_kernel,
        grid=(batch, nb),
        in_specs=[pl.BlockSpec((tb, width), lambda b, i: (b * nb + i, 0)),
                  pl.BlockSpec((hal, width), lambda b, i: (jnp.maximum((b * nb + i) * (tb // hal) - 1, 0), 0)),
                  pl.BlockSpec(w_grp.shape, lambda b, i: (0, 0, 0)),
                  pl.BlockSpec((1, width), lambda b, i: (0, 0))],
        out_specs=pl.BlockSpec((tb, width), lambda b, i: (b * nb + i, 0)),
        out_shape=jax.ShapeDtypeStruct((n_rows_out, width), F32),
        scratch_shapes=[pltpu.VMEM((hal + tb, width), F32)],
        compiler_params=_cparams(("parallel", "parallel")),
        name="pool_prompt",
    )(ub, ub, w_grp, scale)


def _gdn_kernel(x_ref, halo_ref, z_ref, gcol_ref, cw_ref, nw_ref, o_ref, s_out_ref, s_ref, buf_ref,
                *, n_heads, dk, dv, col_g, col_beta):
    i = pl.program_id(1)
    tb = x_ref.shape[0]
    c = GDN_CHUNK
    hal = SUBLANES

    @pl.when(i == 0)
    def _():
        s_ref[...] = jnp.zeros(s_ref.shape, F32)

    buf_ref[0:hal, :] = jnp.where(i == 0, 0.0, halo_ref[...])
    buf_ref[hal:, :] = x_ref[...]
    y = buf_ref[hal:hal + tb, :] * cw_ref[CONV_W - 1:CONV_W, :]
    for j in range(CONV_W - 1):
        off = hal - (CONV_W - 1) + j
        y = y + buf_ref[off:off + tb, :] * cw_ref[j:j + 1, :]
    qkv = _silu(y)

    rr = n_heads * c
    ri = lax.broadcasted_iota(jnp.int32, (rr, rr), 0)
    ci = lax.broadcasted_iota(jnp.int32, (rr, rr), 1)
    same = (ri // c) == (ci // c)
    incl = jnp.logical_and(same, ci <= ri)
    strict = jnp.logical_and(same, ci < ri)
    r1 = lax.broadcasted_iota(jnp.int32, (c, c), 0)
    c1 = lax.broadcasted_iota(jnp.int32, (c, c), 1)
    tril01 = jnp.where(c1 <= r1, 1.0, 0.0).astype(BF16)
    lane = lax.broadcasted_iota(jnp.int32, (rr, LANES), 1)
    n_steps = max(1, (c - 1).bit_length())

    def stack(parts):
        return jnp.concatenate(parts, axis=0)

    for ch in range(tb // c):
        rows = pl.ds(ch * c, c)
        lo_r, hi_r = ch * c, (ch + 1) * c
        gcol = gcol_ref[rows, :]
        decay_col = _dot_exact_lhs01(tril01, gcol)
        qs, ks, vs, betas, dcols, dlasts = [], [], [], [], [], []
        for h in range(n_heads):
            q = qkv[lo_r:hi_r, h * dk:(h + 1) * dk]
            k = qkv[lo_r:hi_r, (n_heads + h) * dk:(n_heads + h + 1) * dk]
            qs.append(q * lax.rsqrt(jnp.sum(q * q, axis=-1, keepdims=True) + L2_EPS) * dk ** -0.5)
            ks.append(k * lax.rsqrt(jnp.sum(k * k, axis=-1, keepdims=True) + L2_EPS))
            vs.append(qkv[lo_r:hi_r, 2 * n_heads * dk + h * dv:2 * n_heads * dk + (h + 1) * dv])
            betas.append(gcol[:, col_beta + h:col_beta + h + 1])
            dcols.append(decay_col[:, col_g + h:col_g + h + 1])
            dlasts.append(jnp.broadcast_to(decay_col[c - 1:c, col_g + h:col_g + h + 1], (c, 1)))
        q, k, v = stack(qs), stack(ks), stack(vs)
        beta, dcol, dlast = stack(betas), stack(dcols), stack(dlasts)

        d_hi, d_mid, d_lo = (t.astype(F32) for t in _split3(dcol))
        a_mat = jnp.where(lane == 0, d_hi, jnp.where(lane == 1, d_mid, jnp.where(lane == 2, d_lo,
                          jnp.where(lane < 6, 1.0, 0.0))))
        b_mat = jnp.where(lane == 3, -d_hi, jnp.where(lane == 4, -d_mid, jnp.where(lane == 5, -d_lo,
                          jnp.where(lane < 3, 1.0, 0.0))))
        gam = jnp.exp(jnp.where(incl, _dot_nt(a_mat.astype(BF16), b_mat.astype(BF16)), NEG_BIG))

        kb = k * beta
        kbf = k.astype(BF16)
        m = jnp.where(strict, _dot_nt(kb.astype(BF16), kbf) * gam, 0.0)
        qk = _dot_nt(q.astype(BF16), kbf) * gam
        ed = jnp.exp(dcol)
        x = jnp.concatenate([v * beta, kb * ed], axis=-1)
        p = -m
        for st in range(n_steps):
            x = x + _dot_rhs_hi(p, x)
            if st + 1 < n_steps:
                p = _dot_rhs_hi(p, p)
        value = x[:, :dv]
        kcum = x[:, dv:].astype(BF16)
        qd = (q * ed).astype(BF16)
        kd = (k * jnp.exp(dlast - dcol)).astype(BF16)
        vnews, o_state = [], []
        for h in range(n_heads):
            hr = slice(h * c, (h + 1) * c)
            s = s_ref[h]
            sb = s.astype(BF16)
            vnew = value[hr] - _dot(kcum[hr], sb)
            vnews.append(vnew)
            o_state.append(_dot(qd[hr], sb))
            s_ref[h] = s * jnp.exp(dlast[h * c:h * c + 1, :]) + _dot_tn(kd[hr], vnew.astype(BF16))
        o = stack(o_state) + _dot(qk.astype(BF16), stack(vnews).astype(BF16))
        o = o * lax.rsqrt(jnp.mean(o * o, axis=-1, keepdims=True) + RMS_EPS)
        for h in range(n_heads):
            zz = z_ref[rows, pl.ds(h * dv, dv)]
            o_ref[rows, pl.ds(h * dv, dv)] = o[h * c:(h + 1) * c] * nw_ref[...] * _silu(zz)

    @pl.when(i == pl.num_programs(1) - 1)
    def _():
        s_out_ref[...] = s_ref[...]


def _gdn_prompt(qkvc, zc, gcol, conv_w, onorm_w, n_rows_out, batch, seq, n_heads, dk, dv, col_g, col_beta, tb):
    nb = seq // tb
    wq = qkvc.shape[1]
    wz = zc.shape[1]
    kern = functools.partial(_gdn_kernel, n_heads=n_heads, dk=dk, dv=dv, col_g=col_g, col_beta=col_beta)
    return pl.pallas_call(
        kern,
        grid=(batch, nb),
        in_specs=[pl.BlockSpec((tb, wq), lambda b, i: (b * nb + i, 0)),
                  pl.BlockSpec((SUBLANES, wq), lambda b, i: (jnp.maximum((b * nb + i) * (tb // SUBLANES) - 1, 0), 0)),
                  pl.BlockSpec((tb, wz), lambda b, i: (b * nb + i, 0)),
                  pl.BlockSpec((tb, LANES), lambda b, i: (b * nb + i, 0)),
                  pl.BlockSpec((CONV_W, wq), lambda b, i: (0, 0)),
                  pl.BlockSpec((1, dv), lambda b, i: (0, 0))],
        out_specs=[pl.BlockSpec((tb, wz), lambda b, i: (b * nb + i, 0)),
                   pl.BlockSpec((None, n_heads, dk, dv), lambda b, i: (b, 0, 0, 0))],
        out_shape=[jax.ShapeDtypeStruct((n_rows_out, wz), F32),
                   jax.ShapeDtypeStruct((batch, n_heads, dk, dv), F32)],
        scratch_shapes=[pltpu.VMEM((n_heads, dk, dv), F32), pltpu.VMEM((SUBLANES + tb, wq), F32)],
        compiler_params=_cparams(("parallel", "arbitrary")),
        name="gdn_prompt",
    )(qkvc, qkvc, zc, gcol, conv_w, onorm_w)


def _fox_decode_kernel(pt_ref, q_ref, kn_ref, vn_ref, lfn_ref, *refs, layer, n_pages, head_dim):
    lf_refs = refs[:n_pages]
    k_hbm, v_hbm, o_ref, kbuf, vbuf, sem = refs[n_pages:]
    b = pl.program_id(0)
    slot = b % 2
    _, _, n_heads, _, page = kbuf.shape

    def start(seq, s):
        for j in range(n_pages):
            pg = pt_ref[seq * n_pages + j]
            pltpu.make_async_copy(k_hbm.at[layer, pg], kbuf.at[s, j], sem.at[0, s]).start()
            pltpu.make_async_copy(v_hbm.at[layer, pg], vbuf.at[s, j], sem.at[1, s]).start()

    @pl.when(b == 0)
    def _():
        start(0, 0)

    @pl.when(b + 1 < pl.num_programs(0))
    def _():
        start(b + 1, 1 - slot)

    pltpu.make_async_copy(k_hbm.at[layer, pl.ds(0, n_pages)], kbuf.at[slot], sem.at[0, slot]).wait()
    pltpu.make_async_copy(v_hbm.at[layer, pl.ds(0, n_pages)], vbuf.at[slot], sem.at[1, slot]).wait()

    qt = q_ref[...] * head_dim ** -0.5
    knt = kn_ref[...]
    vnt = vn_ref[...]
    lfn = lfn_ref[...]
    r = lax.broadcasted_iota(jnp.int32, (page, page), 0)
    c = lax.broadcasted_iota(jnp.int32, (page, page), 1)
    later01 = jnp.where(r > c, 1.0, 0.0).astype(BF16)

    later = jnp.zeros((n_heads, 1), F32)
    rps = [None] * n_pages
    for j in reversed(range(n_pages)):
        lf = lf_refs[j][...]
        rps[j] = _dot_exact_rhs01(lf, later01) + later
        later = later + jnp.sum(lf, axis=-1, keepdims=True)

    for h in range(n_heads):
        qcol = qt[:, h:h + 1]
        qb = jnp.broadcast_to(qcol, (head_dim, page))
        s_new = jnp.sum(qcol * knt[:, h:h + 1], axis=0, keepdims=True) - lfn[:, h:h + 1]
        m = s_new
        scores = []
        for j in range(n_pages):
            s = jnp.sum(kbuf[slot, j, h] * qb, axis=0, keepdims=True) + rps[j][h:h + 1, :]
            scores.append(s)
            m = jnp.maximum(m, jnp.max(s, axis=-1, keepdims=True))
        p_new = jnp.exp(s_new - m)
        denom = p_new
        acc = jnp.zeros((head_dim, page), F32)
        for j in range(n_pages):
            p = jnp.exp(scores[j] - m)
            denom = denom + jnp.sum(p, axis=-1, keepdims=True)
            acc = acc + vbuf[slot, j, h] * p
        o = jnp.sum(acc, axis=-1, keepdims=True) + p_new * vnt[:, h:h + 1]
        o_ref[:, h:h + 1] = o / denom


def _fox_decode(layer, page_table, qt, knt, vnt, lfn, cache_kt, cache_vt, lft):
    bd, n_pages = page_table.shape
    _, _, n_heads, head_dim, page = cache_kt.shape
    kern = functools.partial(_fox_decode_kernel, layer=layer, n_pages=n_pages, head_dim=head_dim)
    col = pl.BlockSpec((None, head_dim, n_heads), lambda b, pt: (b, 0, 0))

    def lf_spec(j):
        return pl.BlockSpec((None, None, n_heads, page), lambda b, pt: (layer, pt[b * n_pages + j], 0, 0))

    in_specs = ([col, col, col, pl.BlockSpec((None, 1, n_heads), lambda b, pt: (b, 0, 0))]
                + [lf_spec(j) for j in range(n_pages)]
                + [pl.BlockSpec(memory_space=pl.ANY), pl.BlockSpec(memory_space=pl.ANY)])
    return pl.pallas_call(
        kern,
        grid_spec=pltpu.PrefetchScalarGridSpec(
            num_scalar_prefetch=1, grid=(bd,), in_specs=in_specs,
            out_specs=pl.BlockSpec((None, head_dim, n_heads), lambda b, pt: (b, 0, 0)),
            scratch_shapes=[pltpu.VMEM((2, n_pages, n_heads, head_dim, page), F32),
                            pltpu.VMEM((2, n_pages, n_heads, head_dim, page), F32),
                            pltpu.SemaphoreType.DMA((2, 2))]),
        out_shape=jax.ShapeDtypeStruct((bd, head_dim, n_heads), F32),
        compiler_params=_cparams(("arbitrary",)),
        name="fox_decode",
    )(page_table.reshape(-1), qt, knt, vnt, lfn, *([lft] * n_pages), cache_kt, cache_vt)


def _pool_decode_kernel(st_ref, x_ref, w_ref, scale_ref, o_ref, new_ref, *, pos0):
    width = x_ref.shape[1]
    n_buf = st_ref.shape[1] // width
    for g, w in enumerate(POOL_WINDOWS):
        sl = pl.ds(g * LANES, LANES)
        win = x_ref[:, sl]
        for j in range(1, w):
            win = win + st_ref[:, pl.ds((n_buf - j) * width + g * LANES, LANES)]
        z = win / float(min(pos0 + 1, w)) - x_ref[:, sl]
        o_ref[:, sl] = _dot(z.astype(BF16), w_ref[g]) * scale_ref[:, sl]
    new_ref[:, 0:(n_buf - 1) * width] = st_ref[:, width:]
    new_ref[:, (n_buf - 1) * width:] = x_ref[...]


def _pool_decode(state_flat, x, w_grp, scale, pos0):
    bd, width = x.shape
    kern = functools.partial(_pool_decode_kernel, pos0=pos0)
    full = lambda a: pl.BlockSpec(a.shape, lambda i: (0,) * a.ndim)
    return pl.pallas_call(
        kern,
        grid=(1,),
        in_specs=[full(state_flat), full(x), full(w_grp), full(scale)],
        out_specs=[full(x), full(state_flat)],
        out_shape=[jax.ShapeDtypeStruct(x.shape, F32), jax.ShapeDtypeStruct(state_flat.shape, F32)],
        compiler_params=_cparams(("arbitrary",)),
        name="pool_decode",
    )(state_flat, x, w_grp, scale)


def _gdn_decode_kernel(x_ref, cs_ref, z_ref, g_ref, s_ref, cw_ref, nw_ref, o_ref, nc_ref, so_ref,
                       *, n_heads, dk, dv, col_g, col_beta):
    x = x_ref[...]
    cs = cs_ref[...]
    y = x * cw_ref[CONV_W - 1:CONV_W, :]
    for j in range(CONV_W - 1):
        y = y + cs[j:j + 1, :] * cw_ref[j:j + 1, :]
    qkv = _silu(y)
    nc_ref[0:CONV_W - 2, :] = cs[1:, :]
    nc_ref[CONV_W - 2:CONV_W - 1, :] = x
    gates = g_ref[...]
    for h in range(n_heads):
        q = qkv[:, h * dk:(h + 1) * dk]
        k = qkv[:, (n_heads + h) * dk:(n_heads + h + 1) * dk]
        v = qkv[:, 2 * n_heads * dk + h * dv:2 * n_heads * dk + (h + 1) * dv]
        q = q * lax.rsqrt(jnp.sum(q * q, axis=-1, keepdims=True) + L2_EPS) * dk ** -0.5
        k = k * lax.rsqrt(jnp.sum(k * k, axis=-1, keepdims=True) + L2_EPS)
        kc = jnp.transpose(jnp.broadcast_to(k, (dk, dk)))
        qc = jnp.transpose(jnp.broadcast_to(q, (dk, dk)))
        a = jnp.exp(gates[:, col_g + h:col_g + h + 1])
        beta = gates[:, col_beta + h:col_beta + h + 1]
        s = s_ref[h] * a
        u = (v - jnp.sum(kc * s, axis=0, keepdims=True)) * beta
        s = s + kc * u
        so_ref[h] = s
        o = jnp.sum(qc * s, axis=0, keepdims=True)
        o = o * lax.rsqrt(jnp.mean(o * o, axis=-1, keepdims=True) + RMS_EPS)
        o_ref[:, pl.ds(h * dv, dv)] = o * nw_ref[...] * _silu(z_ref[:, pl.ds(h * dv, dv)])


def _gdn_decode(x3, conv_state, z3, g3, s_state, conv_w, onorm_w, col_g, col_beta):
    bd, n_heads, dk, dv = s_state.shape
    wq = x3.shape[-1]
    wz = z3.shape[-1]
    kern = functools.partial(_gdn_decode_kernel, n_heads=n_heads, dk=dk, dv=dv, col_g=col_g, col_beta=col_beta)
    per_seq = lambda a: pl.BlockSpec((None,) + a.shape[1:], lambda b: (b,) + (0,) * (a.ndim - 1))
    full = lambda a: pl.BlockSpec(a.shape, lambda b: (0,) * a.ndim)
    return pl.pallas_call(
        kern,
        grid=(bd,),
        in_specs=[per_seq(x3), per_seq(conv_state), per_seq(z3), per_seq(g3), per_seq(s_state),
                  full(conv_w), full(onorm_w)],
        out_specs=[per_seq(z3), per_seq(conv_state), per_seq(s_state)],
        out_shape=[jax.ShapeDtypeStruct(z3.shape, F32), jax.ShapeDtypeStruct(conv_state.shape, F32),
                   jax.ShapeDtypeStruct(s_state.shape, F32)],
        compiler_params=_cparams(("parallel",)),
        name="gdn_decode",
    )(x3, conv_state, z3, g3, s_state, conv_w, onorm_w)


def _merge_kernel(oa_ref, ob_ref, oc_ref, gl_ref, x_ref, wb_ref, wo_ref, g_ref, b_ref, rw_ref, rb_ref,
                  x1_ref, x1t_ref, gate_ref, idx_ref, rank_ref, counts_ref, cnt_ref, *, alpha, n_experts):
    d = x_ref.shape[1]
    m = None
    for bi, o_ref in enumerate((oa_ref, ob_ref, oc_ref)):
        proj = _dot(o_ref[...].astype(BF16), wb_ref[bi])
        term = _sigmoid(gl_ref[:, pl.ds(bi * d, d)]) * proj
        m = term if m is None else m + term
    h = alpha * x_ref[...] + _dot(m.astype(BF16), wo_ref[...])
    x1 = _layer_norm(h, g_ref[...], b_ref[...])
    x1_ref[...] = x1
    _rows_to_tiles(x1t_ref, x1, x1.shape[0], d // LANES)
    logits = _dot_hi(x1, rw_ref[...]) + rb_ref[...]
    lane = lax.broadcasted_iota(jnp.int32, logits.shape, 1)
    lg = jnp.where(lane < n_experts, logits, NEG_BIG)
    vals = []
    picks = []
    gate = jnp.zeros(logits.shape, F32)
    idx = jnp.zeros(logits.shape, jnp.int32)
    for k in range(TOP_K):
        mx = jnp.max(lg, axis=-1, keepdims=True)
        ix = jnp.min(jnp.where(lg == mx, lane, LANES), axis=-1, keepdims=True)
        vals.append(mx)
        picks.append(lane == ix)
        idx = jnp.where(lane == k, ix, idx)
        lg = jnp.where(lane == ix, NEG_BIG, lg)
    exps = [jnp.exp(v - vals[0]) for v in vals]
    tot = exps[0]
    for e in exps[1:]:
        tot = tot + e
    for k in range(TOP_K):
        gate = jnp.where(lane == k, exps[k] / tot, gate)
    gate_ref[...] = gate
    idx_ref[...] = idx

    @pl.when(pl.program_id(0) == 0)
    def _():
        cnt_ref[...] = jnp.zeros(cnt_ref.shape, F32)

    tm = logits.shape[0]
    chosen = jnp.zeros(logits.shape, F32)
    for pk in picks:
        chosen = jnp.where(pk, 1.0, chosen)
    r = lax.broadcasted_iota(jnp.int32, (tm, tm), 0)
    c = lax.broadcasted_iota(jnp.int32, (tm, tm), 1)
    before = _dot(jnp.where(c < r, 1.0, 0.0).astype(BF16), chosen.astype(BF16)) + cnt_ref[...]
    rank = jnp.zeros(logits.shape, F32)
    for k, pk in enumerate(picks):
        rank = jnp.where(lane == k, jnp.sum(jnp.where(pk, before, 0.0), axis=-1, keepdims=True), rank)
    rank_ref[...] = rank.astype(jnp.int32)
    cnt_ref[...] = cnt_ref[...] + jnp.sum(chosen, axis=0, keepdims=True)
    counts_ref[...] = cnt_ref[...]


def _merge(oa, ob, oc, gl, x, wb, wo, ln_g, ln_b, rw, rb, alpha, n_experts, tm):
    nt, d = x.shape
    w = oa.shape[1]
    kern = functools.partial(_merge_kernel, alpha=alpha, n_experts=n_experts)
    rows = lambda width: pl.BlockSpec((tm, width), lambda i: (i, 0))
    full = lambda a: pl.BlockSpec(a.shape, lambda i: (0,) * a.ndim)
    return pl.pallas_call(
        kern,
        grid=(nt // tm,),
        in_specs=[rows(w), rows(w), rows(w), rows(gl.shape[1]), rows(d), full(wb), full(wo), full(ln_g),
                  full(ln_b), full(rw), full(rb)],
        out_specs=[rows(d), pl.BlockSpec((tm * (d // LANES), LANES), lambda i: (i, 0)), rows(LANES), rows(LANES),
                   rows(LANES), pl.BlockSpec((1, LANES), lambda i: (0, 0))],
        out_shape=[jax.ShapeDtypeStruct((nt, d), F32), jax.ShapeDtypeStruct((nt * (d // LANES), LANES), F32),
                   jax.ShapeDtypeStruct((nt, LANES), F32),
                   jax.ShapeDtypeStruct((nt, LANES), jnp.int32), jax.ShapeDtypeStruct((nt, LANES), jnp.int32),
                   jax.ShapeDtypeStruct((1, LANES), F32)],
        scratch_shapes=[pltpu.VMEM((1, LANES), F32)],
        compiler_params=_cparams(("arbitrary",)),
        name="merge_ln_router",
    )(oa, ob, oc, gl, x, wb, wo, ln_g, ln_b, rw, rb)


def _moe_kernel(blk_e_ref, tok0_ref, tok_next_ref, dst_prev_ref, x_hbm, wu_ref, bu_ref, wd_ref, bd_ref,
                out_hbm, xs0_ref, xs1_ref, ys0_ref, ys1_ref, wub_ref, wdb_ref, gsem, ssem, *, blk):
    i = pl.program_id(0)
    ds = xs0_ref.shape[0] // blk
    dff = wd_ref.shape[0]
    xs = (xs0_ref, xs1_ref)
    ys = (ys0_ref, ys1_ref)

    def tile(ref, start):
        return ref.at[pl.ds(pl.multiple_of(start, ds), ds), :]

    def gather(idx_ref, s):
        for r in range(blk):
            pltpu.make_async_copy(tile(x_hbm, idx_ref[0, r]), xs[s].at[pl.ds(r * ds, ds), :], gsem.at[s]).start()

    def wait_gather(s):
        pltpu.make_async_copy(x_hbm.at[pl.ds(0, blk * ds), :], xs[s], gsem.at[s]).wait()

    def wait_scatter(s):
        pltpu.make_async_copy(ys[s], out_hbm.at[pl.ds(0, blk * ds), :], ssem.at[s]).wait()

    @pl.when(i == 0)
    def _():
        ys0_ref[...] = jnp.zeros(ys0_ref.shape, F32)
        ys1_ref[...] = jnp.zeros(ys1_ref.shape, F32)
        gather(tok0_ref, 0)

    @pl.when(jnp.logical_or(i == 0, blk_e_ref[i] != blk_e_ref[jnp.maximum(i - 1, 0)]))
    def _():
        wub_ref[...] = wu_ref[...].astype(BF16)
        wdb_ref[...] = wd_ref[...].astype(BF16)

    def step(slot):
        other = 1 - slot
        wait_gather(slot)

        @pl.when(i >= 1)
        def _():
            wait_scatter(slot)

        xb = _tiles_to_rows(xs[slot], blk, ds).astype(BF16)
        hb = _dot(xb, wub_ref[...]) + bu_ref[...]
        gt = jnp.minimum(hb[:, :dff], SWIGLU_LIMIT)
        up = jnp.clip(hb[:, dff:], -SWIGLU_LIMIT, SWIGLU_LIMIT)
        act = (up + 1.0) * (gt * _sigmoid(SWIGLU_ALPHA * gt))
        _rows_to_tiles(ys[slot], _dot(act.astype(BF16), wdb_ref[...]) + bd_ref[...], blk, ds)

        gather(tok_next_ref, other)
        for r in range(blk):
            pltpu.make_async_copy(ys[other].at[pl.ds(r * ds, ds), :], tile(out_hbm, dst_prev_ref[0, r]),
                                  ssem.at[other]).start()

        @pl.when(i == pl.num_programs(0) - 1)
        def _():
            wait_gather(other)
            wait_scatter(other)

    @pl.when(i % 2 == 0)
    def _():
        step(0)

    @pl.when(i % 2 == 1)
    def _():
        step(1)


def _moe(x1t, blk_e, row_tok, row_dst, wu, bu, wd, bd, n_out_rows):
    n_blk = row_tok.shape[0]
    blk = row_tok.shape[-1]
    n_exp, d, dff2 = wu.shape
    ds = d // LANES
    idx_spec = lambda f: pl.BlockSpec((None, 1, blk), f, memory_space=pltpu.SMEM)
    return pl.pallas_call(
        functools.partial(_moe_kernel, blk=blk),
        grid_spec=pltpu.PrefetchScalarGridSpec(
            num_scalar_prefetch=1, grid=(n_blk + 1,),
            in_specs=[idx_spec(lambda i, be: (0, 0, 0)),
                      idx_spec(lambda i, be: (jnp.minimum(i + 1, n_blk - 1), 0, 0)),
                      idx_spec(lambda i, be: (i, 0, 0)),
                      pl.BlockSpec(memory_space=pl.ANY),
                      pl.BlockSpec((None, d, dff2), lambda i, be: (be[i], 0, 0)),
                      pl.BlockSpec((None, 1, dff2), lambda i, be: (be[i], 0, 0)),
                      pl.BlockSpec((None, dff2 // 2, d), lambda i, be: (be[i], 0, 0)),
                      pl.BlockSpec((None, 1, d), lambda i, be: (be[i], 0, 0))],
            out_specs=pl.BlockSpec(memory_space=pl.ANY),
            scratch_shapes=[pltpu.VMEM((blk * ds, LANES), F32), pltpu.VMEM((blk * ds, LANES), F32),
                            pltpu.VMEM((blk * ds, LANES), F32), pltpu.VMEM((blk * ds, LANES), F32),
                            pltpu.VMEM((d, dff2), BF16), pltpu.VMEM((dff2 // 2, d), BF16),
                            pltpu.SemaphoreType.DMA((2,)), pltpu.SemaphoreType.DMA((2,))]),
        out_shape=jax.ShapeDtypeStruct((n_out_rows * ds, LANES), F32),
        compiler_params=_cparams(("arbitrary",)),
        name="moe_experts",
    )(blk_e, row_tok, row_tok, row_dst, x1t, wu, bu, wd, bd)


def _route(idx, rank, counts, n_experts, nt, tile_rows):
    tk = nt * TOP_K
    pcounts = (counts + MOE_BLOCK - 1) // MOE_BLOCK * MOE_BLOCK
    pend = jnp.cumsum(pcounts)
    pstart = pend - pcounts
    experts = jnp.arange(n_experts, dtype=jnp.int32)
    dest = rank + jnp.sum(jnp.where(idx[:, :, None] == experts, pstart, 0), axis=-1)
    n_rows = -(-(tk + n_experts * (MOE_BLOCK - 1)) // MOE_BLOCK) * MOE_BLOCK
    n_blk = n_rows // MOE_BLOCK
    row_pair = jnp.full((n_rows,), -1, jnp.int32).at[dest.reshape(-1)].set(jnp.arange(tk, dtype=jnp.int32))
    dump = TOP_K * nt + jnp.arange(n_rows, dtype=jnp.int32) % MOE_BLOCK
    row_tok = jnp.where(row_pair >= 0, row_pair // TOP_K, 0)
    row_dst = jnp.where(row_pair >= 0, (row_pair % TOP_K) * nt + row_pair // TOP_K, dump)
    row_dst = jnp.concatenate([dump[:MOE_BLOCK], row_dst])
    blk_start = jnp.arange(n_blk, dtype=jnp.int32) * MOE_BLOCK
    blk_e = jnp.minimum(jnp.sum((pend[None, :] <= blk_start[:, None]).astype(jnp.int32), axis=1), n_experts - 1)
    blk_e = jnp.concatenate([blk_e, blk_e[-1:]])
    return (blk_e, (row_tok * tile_rows).reshape(n_blk, 1, MOE_BLOCK),
            (row_dst * tile_rows).reshape(n_blk + 1, 1, MOE_BLOCK))


def _ple_kernel(x1_ref, y0_ref, y1_ref, y2_ref, y3_ref, gate_ref, p_ref, wg_ref, wp_ref, g_ref, b_ref, o_ref, *, alpha):
    gate = gate_ref[...]
    tm, d = x1_ref.shape
    f = None
    for k, y_ref in enumerate((y0_ref, y1_ref, y2_ref, y3_ref)):
        term = gate[:, k:k + 1] * _tiles_to_rows(y_ref, tm, d // LANES)
        f = term if f is None else f + term
    h = alpha * x1_ref[...] + f
    gpe = _sigmoid(_dot(h.astype(BF16), wg_ref[...])) * _dot(p_ref[...].astype(BF16), wp_ref[...])
    o_ref[...] = _layer_norm(h + gpe, g_ref[...], b_ref[...])


def _ple(x1, out4, gate, p, wg, wp, ln_g, ln_b, alpha, tm):
    nt, d = x1.shape
    nb = nt // tm
    kern = functools.partial(_ple_kernel, alpha=alpha)
    rows = lambda width: pl.BlockSpec((tm, width), lambda i: (i, 0))
    full = lambda a: pl.BlockSpec(a.shape, lambda i: (0,) * a.ndim)
    slot = lambda k: pl.BlockSpec((tm * (d // LANES), LANES), lambda i: (k * nb + i, 0))
    return pl.pallas_call(
        kern,
        grid=(nb,),
        in_specs=[rows(d), slot(0), slot(1), slot(2), slot(3), rows(LANES), rows(p.shape[1]), full(wg), full(wp),
                  full(ln_g), full(ln_b)],
        out_specs=rows(d),
        out_shape=jax.ShapeDtypeStruct((nt, d), F32),
        compiler_params=_cparams(("parallel",)),
        name="combine_ple_ln",
    )(x1, out4, out4, out4, out4, gate, p, wg, wp, ln_g, ln_b)


def _row_tile(n, cap):
    best = None
    for t in range(LANES, cap + 1, LANES):
        if n % t == 0:
            best = t
    assert best is not None, n
    return best


def kernel(x_prompt, x_sample, cache_k, cache_v, cache_logf, state_pool, state_conv, state_delta, page_table,
           p_prompt, p_sample, w_in, b_f, w_grp, pool_scale, conv_w, a_log, dt_bias, onorm_w, w_branch, w_o,
           ln1_g, ln1_b, router_w, router_b, w_up, b_up, w_down, b_down, w_pe, w_peg, ln2_g, ln2_b):
    batch, seq, d = x_prompt.shape
    bd, dec_seq, _ = x_sample.shape
    assert dec_seq == 1
    depth = w_in.shape[0]
    _, _, page, n_heads, head_dim = cache_k.shape
    width = n_heads * head_dim
    _, _, gh, dk, dv = state_delta.shape
    gqkv = state_conv.shape[3]
    n_branch = w_branch.shape[1]
    n_exp = router_w.shape[2]
    ple_dim = p_prompt.shape[3]
    n_buf = state_pool.shape[2]
    past = page_table.shape[1] * page
    alpha = (2 * depth) ** 0.25
    n_p = batch * seq
    nt = n_p + bd
    tm = _row_tile(nt, 512)

    sizes = (width, width, width, n_heads, width, gqkv, gh, gh, width, n_branch * d)
    offs = [0]
    for s in sizes:
        offs.append(offs[-1] + s)
    o_q, _, _, o_f, o_ub, o_c, o_a, o_b, o_z, o_gl, o_end = offs
    assert o_end == w_in.shape[2]
    col_g, col_beta = n_heads, n_heads + gh

    x = jnp.concatenate([x_prompt.reshape(n_p, d), x_sample.reshape(bd, d)], axis=0)
    lft = jnp.swapaxes(cache_logf, 2, 3)
    cache_kt = jnp.transpose(cache_k, (0, 1, 3, 4, 2))
    cache_vt = jnp.transpose(cache_v, (0, 1, 3, 4, 2))
    rows_p, rows_s = [], []
    for l in range(depth):
        wt = jnp.swapaxes(w_in[l], 0, 1)
        rows = lambda a, n: wt[a:a + n].astype(BF16)
        wt_small = jnp.concatenate([wt[o_f:o_f + n_heads], wt[o_a:o_a + gh], wt[o_b:o_b + gh]], axis=0)
        bias = jnp.concatenate([b_f[l], dt_bias[l], jnp.zeros((gh,), F32)])
        alog = jnp.concatenate([jnp.zeros((n_heads,), F32), a_log[l], jnp.zeros((gh,), F32)])
        qa, ka, va, ub, qkvc, zc = _project(x, [rows(o_q, width), rows(o_q + width, width),
                                                rows(o_q + 2 * width, width), rows(o_ub, width),
                                                rows(o_c, gqkv), rows(o_z, width)], tm)
        gl, = _project(x, [rows(o_gl, n_branch * d)], tm)
        gcol, gt = _small_proj(x, wt_small, bias, alog, n_heads, gh, tm)
        ct = _cumsum_rows(gt, batch, seq)

        wg_bf = w_grp[l].astype(BF16)
        pscale = pool_scale[l].reshape(1, width)
        onw = onorm_w[l].reshape(1, dv)
        oa = _fox_prompt(qa, ka, va, ct, n_p, batch, seq, n_heads, head_dim, min(seq, 512))
        ob = _pool_prompt(ub, wg_bf, pscale, n_p, batch, seq, min(seq, 256))
        oc, s_p = _gdn_prompt(qkvc, zc, gcol, conv_w[l], onw, n_p, batch, seq, gh, dk, dv, col_g, col_beta,
                              min(seq, 4 * GDN_CHUNK))

        q3 = qa[n_p:].reshape(bd, n_heads, head_dim)
        kn3 = ka[n_p:].reshape(bd, n_heads, head_dim)
        vn3 = va[n_p:].reshape(bd, n_heads, head_dim)
        lfn = gcol[n_p:, :n_heads].reshape(bd, 1, n_heads)
        to_cols = lambda a: jnp.swapaxes(a, 1, 2)
        oa_s = to_cols(_fox_decode(l, page_table, to_cols(q3), to_cols(kn3), to_cols(vn3), lfn,
                                   cache_kt, cache_vt, lft)).reshape(bd, width)
        ob_s, pool_new = _pool_decode(state_pool[l].reshape(bd, n_buf * width), ub[n_p:], wg_bf, pscale, past)
        oc_s, conv_new, s_new = _gdn_decode(qkvc[n_p:].reshape(bd, 1, gqkv), state_conv[l],
                                            zc[n_p:].reshape(bd, 1, width), gcol[n_p:].reshape(bd, 1, LANES),
                                            state_delta[l], conv_w[l], onw, col_g, col_beta)
        oa = jnp.concatenate([oa, oa_s], axis=0)
        ob = jnp.concatenate([ob, ob_s], axis=0)
        oc = jnp.concatenate([oc, oc_s.reshape(bd, width)], axis=0)

        rw = jnp.zeros((d, LANES), F32).at[:, :n_exp].set(router_w[l])
        rb = jnp.zeros((1, LANES), F32).at[0, :n_exp].set(router_b[l])
        x1, x1t, gate, idx, rank, counts = _merge(oa, ob, oc, gl, x, w_branch[l].astype(BF16), w_o[l].astype(BF16),
                                                  ln1_g[l].reshape(1, d), ln1_b[l].reshape(1, d), rw, rb, alpha,
                                                  n_exp, tm)
        blk_e, row_tok, row_dst = _route(idx[:, :TOP_K], rank[:, :TOP_K], counts[0, :n_exp].astype(jnp.int32),
                                         n_exp, nt, d // LANES)
        out4 = _moe(x1t, blk_e, row_tok, row_dst, w_up[l], b_up[l].reshape(n_exp, 1, -1),
                    w_down[l], b_down[l].reshape(n_exp, 1, -1), TOP_K * nt + MOE_BLOCK)
        p = jnp.concatenate([p_prompt[l].reshape(n_p, ple_dim), p_sample[l].reshape(bd, ple_dim)], axis=0)
        x = _ple(x1, out4, gate, p, w_peg[l].astype(BF16), w_pe[l].astype(BF16), ln2_g[l].reshape(1, d),
                 ln2_b[l].reshape(1, d), alpha, tm)

        rows_p.append((ka[:n_p].reshape(batch, seq, n_heads, head_dim),
                       va[:n_p].reshape(batch, seq, n_heads, head_dim),
                       gcol[:n_p, :n_heads].reshape(batch, seq, n_heads),
                       ub[:n_p].reshape(batch, seq, width)[:, seq - n_buf:],
                       qkvc[:n_p].reshape(batch, seq, gqkv)[:, seq - (CONV_W - 1):],
                       s_p))
        rows_s.append((kn3.reshape(bd, 1, n_heads, head_dim), vn3.reshape(bd, 1, n_heads, head_dim),
                       gcol[n_p:, :n_heads].reshape(bd, 1, n_heads), pool_new.reshape(bd, n_buf, width),
                       conv_new, s_new))

    def stk(rows, i):
        return jnp.stack([r[i] for r in rows])

    return (x[:n_p].reshape(batch, seq, d), x[n_p:].reshape(bd, 1, d),
            stk(rows_p, 0), stk(rows_p, 1), stk(rows_p, 2), stk(rows_p, 3), stk(rows_p, 4), stk(rows_p, 5),
            stk(rows_s, 0), stk(rows_s, 1), stk(rows_s, 2), stk(rows_s, 3), stk(rows_s, 4), stk(rows_s, 5))
```

```python
import functools

import jax
import jax.numpy as jnp
from jax import lax
from jax.experimental import pallas as pl
from jax.experimental.pallas import tpu as pltpu

F32 = jnp.float32
BF16 = jnp.bfloat16

POOL_WINDOWS = (2, 4, 8, 16)
CONV_W = 4
GDN_CHUNK = 64
TOP_K = 4
SWIGLU_LIMIT = 7.0
SWIGLU_ALPHA = 1.702
LN_EPS = 1e-5
RMS_EPS = 1e-6
L2_EPS = 1e-6
MOE_BLOCK = 128

LANES = 128
SUBLANES = 8
VMEM_LIMIT = 56 * 1024 * 1024

NEG_BIG = -1e30


def _cparams(sem):
    return pltpu.CompilerParams(dimension_semantics=sem, vmem_limit_bytes=VMEM_LIMIT)


def _split3(a):
    hi = a.astype(BF16)
    r1 = a - hi.astype(F32)
    mid = r1.astype(BF16)
    lo = (r1 - mid.astype(F32)).astype(BF16)
    return hi, mid, lo


def _dot(a, b):
    return jnp.dot(a, b, preferred_element_type=F32)


def _dot_nt(a, b):
    return lax.dot_general(a, b, (((1,), (1,)), ((), ())), preferred_element_type=F32)


def _dot_tn(a, b):
    return lax.dot_general(a, b, (((0,), (0,)), ((), ())), preferred_element_type=F32)


def _dot_exact_rhs01(a, m01):
    hi, mid, lo = _split3(a)
    return _dot(hi, m01) + _dot(mid, m01) + _dot(lo, m01)


def _dot_exact_lhs01(m01, a):
    hi, mid, lo = _split3(a)
    return _dot(m01, hi) + _dot(m01, mid) + _dot(m01, lo)


def _dot_hi(a, b):
    ah = a.astype(BF16)
    al = (a - ah.astype(F32)).astype(BF16)
    bh = b.astype(BF16)
    bl = (b - bh.astype(F32)).astype(BF16)
    return _dot(ah, bh) + _dot(ah, bl) + _dot(al, bh)


def _dot_rhs_hi(a, b):
    ah = a.astype(BF16)
    bh = b.astype(BF16)
    bl = (b - bh.astype(F32)).astype(BF16)
    return _dot(ah, bh) + _dot(ah, bl)


def _sigmoid(x):
    return 1.0 / (1.0 + jnp.exp(-x))


def _softplus(x):
    return jnp.maximum(x, 0.0) + jnp.log1p(jnp.exp(-jnp.abs(x)))


def _silu(x):
    return x * _sigmoid(x)


def _tiles_to_rows(ref, n, ds):
    return jnp.concatenate([ref[pl.ds(s, n, stride=ds), :] for s in range(ds)], axis=-1)


def _rows_to_tiles(ref, val, n, ds):
    for s in range(ds):
        ref[pl.ds(s, n, stride=ds), :] = val[:, s * LANES:(s + 1) * LANES]


def _layer_norm(h, g, b):
    mu = jnp.mean(h, axis=-1, keepdims=True)
    d = h - mu
    var = jnp.mean(d * d, axis=-1, keepdims=True)
    return d * lax.rsqrt(var + LN_EPS) * g + b


def _proj_kernel(x_ref, *refs):
    n = len(refs) // 2
    xb = x_ref[...].astype(BF16)
    for w_ref, o_ref in zip(refs[:n], refs[n:]):
        o_ref[...] = _dot_nt(xb, w_ref[...])


def _project(x, wts, tm):
    m, k = x.shape
    return pl.pallas_call(
        _proj_kernel,
        grid=(m // tm,),
        in_specs=[pl.BlockSpec((tm, k), lambda i: (i, 0))]
                 + [pl.BlockSpec(w.shape, lambda i: (0, 0)) for w in wts],
        out_specs=[pl.BlockSpec((tm, w.shape[0]), lambda i: (i, 0)) for w in wts],
        out_shape=[jax.ShapeDtypeStruct((m, w.shape[0]), F32) for w in wts],
        compiler_params=_cparams(("parallel",)),
        name="proj",
    )(x, *wts)


def _kv_t_kernel(x_ref, wk_ref, wv_ref, kt_ref, vt_ref):
    xb = x_ref[...].astype(BF16)
    kt_ref[...] = _dot_nt(wk_ref[...], xb)
    vt_ref[...] = _dot_nt(wv_ref[...], xb)


def _kv_transposed(x, wk_t, wv_t, batch, seq, ts):
    w, k = wk_t.shape
    ns = seq // ts
    out = jax.ShapeDtypeStruct((batch, w, seq), F32)
    return pl.pallas_call(
        _kv_t_kernel,
        grid=(batch, ns),
        in_specs=[pl.BlockSpec((ts, k), lambda b, j: (b * ns + j, 0)),
                  pl.BlockSpec((w, k), lambda b, j: (0, 0)), pl.BlockSpec((w, k), lambda b, j: (0, 0))],
        out_specs=[pl.BlockSpec((None, w, ts), lambda b, j: (b, 0, j)),
                   pl.BlockSpec((None, w, ts), lambda b, j: (b, 0, j))],
        out_shape=[out, out],
        compiler_params=_cparams(("parallel", "parallel")),
        name="kv_state",
    )(x, wk_t, wv_t)


def _gate_act(s, bias, alog, idx, n_f, n_g):
    z = s + bias
    logf = jnp.minimum(z, 0.0) - jnp.log1p(jnp.exp(-jnp.abs(z)))
    g = -jnp.exp(alog) * _softplus(z)
    beta = _sigmoid(z)
    return jnp.where(idx < n_f, logf, jnp.where(idx < n_f + n_g, g, beta))


def _small_kernel(x_ref, w_ref, wt_ref, bias_ref, alog_ref, biast_ref, alogt_ref, o_ref, ot_ref, *, n_f, n_g):
    xb = x_ref[...].astype(BF16)
    s = _dot_nt(xb, w_ref[...])
    st = _dot_nt(wt_ref[...], xb)
    lane = lax.broadcasted_iota(jnp.int32, s.shape, 1)
    o_ref[...] = _gate_act(s, bias_ref[...], alog_ref[...], lane, n_f, n_g)
    row = lax.broadcasted_iota(jnp.int32, st.shape, 0)
    ot_ref[...] = _gate_act(st, biast_ref[...], alogt_ref[...], row, n_f, n_g)


def _small_proj(x, wt_small, bias, alog, n_f, n_g, tm):
    m, k = x.shape
    nc = wt_small.shape[0]
    wt = wt_small.astype(BF16)
    w_pad = jnp.zeros((LANES, k), BF16).at[:nc].set(wt)
    bias_pad = jnp.zeros((1, LANES), F32).at[0, :nc].set(bias)
    alog_pad = jnp.zeros((1, LANES), F32).at[0, :nc].set(alog)
    kern = functools.partial(_small_kernel, n_f=n_f, n_g=n_g)
    full = lambda shape: pl.BlockSpec(shape, lambda i: (0,) * len(shape))
    return pl.pallas_call(
        kern,
        grid=(m // tm,),
        in_specs=[pl.BlockSpec((tm, k), lambda i: (i, 0)), full((LANES, k)), full((nc, k)),
                  full((1, LANES)), full((1, LANES)), full((nc, 1)), full((nc, 1))],
        out_specs=[pl.BlockSpec((tm, LANES), lambda i: (i, 0)), pl.BlockSpec((nc, tm), lambda i: (0, i))],
        out_shape=[jax.ShapeDtypeStruct((m, LANES), F32), jax.ShapeDtypeStruct((nc, m), F32)],
        compiler_params=_cparams(("parallel",)),
        name="gate_cols",
    )(x, w_pad, wt, bias_pad, alog_pad, bias.reshape(nc, 1), alog.reshape(nc, 1))


CUMSUM_CHUNK = 256


def _cumsum_kernel(gt_ref, ct_ref):
    n = gt_ref.shape[1] // CUMSUM_CHUNK
    r = lax.broadcasted_iota(jnp.int32, (CUMSUM_CHUNK, CUMSUM_CHUNK), 0)
    c = lax.broadcasted_iota(jnp.int32, (CUMSUM_CHUNK, CUMSUM_CHUNK), 1)
    triu = jnp.where(r <= c, 1.0, 0.0).astype(BF16)
    carry = jnp.zeros((gt_ref.shape[0], 1), F32)
    for i in range(n):
        sl = pl.ds(i * CUMSUM_CHUNK, CUMSUM_CHUNK)
        cs = _dot_exact_rhs01(gt_ref[:, sl], triu) + carry
        ct_ref[:, sl] = cs
        carry = cs[:, CUMSUM_CHUNK - 1:CUMSUM_CHUNK]


def _cumsum_rows(gt, n_seq, seq):
    nc = gt.shape[0]
    return pl.pallas_call(
        _cumsum_kernel,
        grid=(n_seq,),
        in_specs=[pl.BlockSpec((nc, seq), lambda b: (0, b))],
        out_specs=pl.BlockSpec((nc, seq), lambda b: (0, b)),
        out_shape=jax.ShapeDtypeStruct((nc, n_seq * seq), F32),
        compiler_params=_cparams(("parallel",)),
        name="logf_cumsum",
    )(gt)


def _fox_kernel(q_ref, k_ref, v_ref, ct_ref, o_ref, m_ref, l_ref, acc_ref, *, n_heads, head_dim):
    qi = pl.program_id(1)
    ki = pl.program_id(2)
    tq = q_ref.shape[0]
    tk = k_ref.shape[0]
    n_pairs = n_heads // 2
    scale = head_dim ** -0.5

    @pl.when(ki == 0)
    def _():
        m_ref[...] = jnp.full(m_ref.shape, NEG_BIG, F32)
        l_ref[...] = jnp.zeros(l_ref.shape, F32)
        acc_ref[...] = jnp.zeros(acc_ref.shape, F32)

    def step(masked):
        lane = lax.broadcasted_iota(jnp.int32, (tq, LANES), 1)
        low = lane < head_dim
        if masked:
            rq = lax.broadcasted_iota(jnp.int32, (tq, tk), 0)
            ck = lax.broadcasted_iota(jnp.int32, (tq, tk), 1)
            causal = ck <= rq
        for p in range(n_pairs):
            sl = pl.ds(p * LANES, LANES)
            qp = (q_ref[:, sl] * scale).astype(BF16)
            kp = k_ref[:, sl].astype(BF16)
            vp = v_ref[:, sl].astype(BF16)
            pv = []
            alphas = []
            for half in range(2):
                h = 2 * p + half
                qh = jnp.where(low if half == 0 else jnp.logical_not(low), qp, jnp.zeros_like(qp))
                s = _dot_nt(qh, kp) - ct_ref[h:h + 1, :]
                if masked:
                    s = jnp.where(causal, s, NEG_BIG)
                m_prev = m_ref[h]
                m_new = jnp.maximum(m_prev, jnp.max(s, axis=-1, keepdims=True))
                alpha = jnp.exp(m_prev - m_new)
                pr = jnp.exp(s - m_new[:, 0:1])
                l_ref[h] = alpha * l_ref[h] + jnp.sum(pr, axis=-1, keepdims=True)
                m_ref[h] = m_new
                pv.append(_dot(pr.astype(BF16), vp))
                alphas.append(alpha)
            a = jnp.where(low, alphas[0], alphas[1])
            acc_ref[p] = a * acc_ref[p] + jnp.where(low, pv[0], pv[1])

    @pl.when(ki < qi)
    def _():
        step(False)

    @pl.when(ki == qi)
    def _():
        step(True)
        lane = lax.broadcasted_iota(jnp.int32, (tq, LANES), 1)
        low = lane < head_dim
        for p in range(n_pairs):
            l = jnp.where(low, l_ref[2 * p], l_ref[2 * p + 1])
            o_ref[:, pl.ds(p * LANES, LANES)] = acc_ref[p] / l


def _fox_prompt(q, k, v, ct, n_rows_out, batch, seq, n_heads, head_dim, tq):
    w = n_heads * head_dim
    nq = seq // tq
    kern = functools.partial(_fox_kernel, n_heads=n_heads, head_dim=head_dim)
    return pl.pallas_call(
        kern,
        grid=(batch, nq, nq),
        in_specs=[pl.BlockSpec((tq, w), lambda b, i, j: (b * nq + i, 0)),
                  pl.BlockSpec((tq, w), lambda b, i, j: (b * nq + jnp.minimum(i, j), 0)),
                  pl.BlockSpec((tq, w), lambda b, i, j: (b * nq + jnp.minimum(i, j), 0)),
                  pl.BlockSpec((n_heads, tq), lambda b, i, j: (0, b * nq + jnp.minimum(i, j)))],
        out_specs=pl.BlockSpec((tq, w), lambda b, i, j: (b * nq + i, 0)),
        out_shape=jax.ShapeDtypeStruct((n_rows_out, w), F32),
        scratch_shapes=[pltpu.VMEM((n_heads, tq, LANES), F32), pltpu.VMEM((n_heads, tq, LANES), F32),
                        pltpu.VMEM((n_heads // 2, tq, LANES), F32)],
        compiler_params=_cparams(("parallel", "parallel", "arbitrary")),
        name="fox_prompt",
    )(q, k, v, ct)


def _pool_kernel(x_ref, halo_ref, w_ref, scale_ref, o_ref, buf_ref):
    i = pl.program_id(1)
    tb = x_ref.shape[0]
    hal = max(POOL_WINDOWS)
    buf_ref[0:hal, :] = jnp.where(i == 0, 0.0, halo_ref[...])
    buf_ref[hal:, :] = x_ref[...]
    pos = (i * tb + lax.broadcasted_iota(jnp.int32, (tb, 1), 0) + 1).astype(F32)
    for g, w in enumerate(POOL_WINDOWS):
        sl = pl.ds(g * LANES, LANES)
        win = buf_ref[hal:hal + tb, sl]
        for j in range(1, w):
            win = win + buf_ref[hal - j:hal - j + tb, sl]
        z = win / jnp.minimum(pos, float(w)) - x_ref[:, sl]
        o_ref[:, sl] = _dot(z.astype(BF16), w_ref[g]) * scale_ref[:, sl]


def _pool_prompt(ub, w_grp, scale, n_rows_out, batch, seq, tb):
    width = ub.shape[1]
    hal = max(POOL_WINDOWS)
    nb = seq // tb
    return pl.pallas_call(
        _pool_kernel,
        grid=(batch, nb),
        in_specs=[pl.BlockSpec((tb, width), lambda b, i: (b * nb + i, 0)),
                  pl.BlockSpec((hal, width), lambda b, i: (jnp.maximum((b * nb + i) * (tb // hal) - 1, 0), 0)),
                  pl.BlockSpec(w_grp.shape, lambda b, i: (0, 0, 0)),
                  pl.BlockSpec((1, width), lambda b, i: (0, 0))],
        out_specs=pl.BlockSpec((tb, width), lambda b, i: (b * nb + i, 0)),
        out_shape=jax.ShapeDtypeStruct((n_rows_out, width), F32),
        scratch_shapes=[pltpu.VMEM((hal + tb, width), F32)],
        compiler_params=_cparams(("parallel", "parallel")),
        name="pool_prompt",
    )(ub, ub, w_grp, scale)


def _gdn_kernel(x_ref, halo_ref, z_ref, gcol_ref, cw_ref, nw_ref, o_ref, s_out_ref, s_ref, buf_ref,
                *, n_heads, dk, dv, col_g, col_beta):
    i = pl.program_id(1)
    tb = x_ref.shape[0]
    c = GDN_CHUNK
    hal = SUBLANES

    @pl.when(i == 0)
    def _():
        s_ref[...] = jnp.zeros(s_ref.shape, F32)

    buf_ref[0:hal, :] = jnp.where(i == 0, 0.0, halo_ref[...])
    buf_ref[hal:, :] = x_ref[...]
    y = buf_ref[hal:hal + tb, :] * cw_ref[CONV_W - 1:CONV_W, :]
    for j in range(CONV_W - 1):
        off = hal - (CONV_W - 1) + j
        y = y + buf_ref[off:off + tb, :] * cw_ref[j:j + 1, :]
    qkv = _silu(y)

    rr = n_heads * c
    ri = lax.broadcasted_iota(jnp.int32, (rr, rr), 0)
    ci = lax.broadcasted_iota(jnp.int32, (rr, rr), 1)
    same = (ri // c) == (ci // c)
    incl = jnp.logical_and(same, ci <= ri)
    strict = jnp.logical_and(same, ci < ri)
    r1 = lax.broadcasted_iota(jnp.int32, (c, c), 0)
    c1 = lax.broadcasted_iota(jnp.int32, (c, c), 1)
    tril01 = jnp.where(c1 <= r1, 1.0, 0.0).astype(BF16)
    lane = lax.broadcasted_iota(jnp.int32, (rr, LANES), 1)
    n_steps = max(1, (c - 1).bit_length())

    def stack(parts):
        return jnp.concatenate(parts, axis=0)

    for ch in range(tb // c):
        rows = pl.ds(ch * c, c)
        lo_r, hi_r = ch * c, (ch + 1) * c
        gcol = gcol_ref[rows, :]
        decay_col = _dot_exact_lhs01(tril01, gcol)
        qs, ks, vs, betas, dcols, dlasts = [], [], [], [], [], []
        for h in range(n_heads):
            q = qkv[lo_r:hi_r, h * dk:(h + 1) * dk]
            k = qkv[lo_r:hi_r, (n_heads + h) * dk:(n_heads + h + 1) * dk]
            qs.append(q * lax.rsqrt(jnp.sum(q * q, axis=-1, keepdims=True) + L2_EPS) * dk ** -0.5)
            ks.append(k * lax.rsqrt(jnp.sum(k * k, axis=-1, keepdims=True) + L2_EPS))
            vs.append(qkv[lo_r:hi_r, 2 * n_heads * dk + h * dv:2 * n_heads * dk + (h + 1) * dv])
            betas.append(gcol[:, col_beta + h:col_beta + h + 1])
            dcols.append(decay_col[:, col_g + h:col_g + h + 1])
            dlasts.append(jnp.broadcast_to(decay_col[c - 1:c, col_g + h:col_g + h + 1], (c, 1)))
        q, k, v = stack(qs), stack(ks), stack(vs)
        beta, dcol, dlast = stack(betas), stack(dcols), stack(dlasts)

        d_hi, d_mid, d_lo = (t.astype(F32) for t in _split3(dcol))
        a_mat = jnp.where(lane == 0, d_hi, jnp.where(lane == 1, d_mid, jnp.where(lane == 2, d_lo,
                          jnp.where(lane < 6, 1.0, 0.0))))
        b_mat = jnp.where(lane == 3, -d_hi, jnp.where(lane == 4, -d_mid, jnp.where(lane == 5, -d_lo,
                          jnp.where(lane < 3, 1.0, 0.0))))
        gam = jnp.exp(jnp.where(incl, _dot_nt(a_mat.astype(BF16), b_mat.astype(BF16)), NEG_BIG))

        kb = k * beta
        kbf = k.astype(BF16)
        m = jnp.where(strict, _dot_nt(kb.astype(BF16), kbf) * gam, 0.0)
        qk = _dot_nt(q.astype(BF16), kbf) * gam
        ed = jnp.exp(dcol)
        x = jnp.concatenate([v * beta, kb * ed], axis=-1)
        p = -m
        for st in range(n_steps):
            x = x + _dot_rhs_hi(p, x)
            if st + 1 < n_steps:
                p = _dot_rhs_hi(p, p)
        value = x[:, :dv]
        kcum = x[:, dv:].astype(BF16)
        qd = (q * ed).astype(BF16)
        kd = (k * jnp.exp(dlast - dcol)).astype(BF16)
        vnews, o_state = [], []
        for h in range(n_heads):
            hr = slice(h * c, (h + 1) * c)
            s = s_ref[h]
            sb = s.astype(BF16)
            vnew = value[hr] - _dot(kcum[hr], sb)
            vnews.append(vnew)
            o_state.append(_dot(qd[hr], sb))
            s_ref[h] = s * jnp.exp(dlast[h * c:h * c + 1, :]) + _dot_tn(kd[hr], vnew.astype(BF16))
        o = stack(o_state) + _dot(qk.astype(BF16), stack(vnews).astype(BF16))
        o = o * lax.rsqrt(jnp.mean(o * o, axis=-1, keepdims=True) + RMS_EPS)
        for h in range(n_heads):
            zz = z_ref[rows, pl.ds(h * dv, dv)]
            o_ref[rows, pl.ds(h * dv, dv)] = o[h * c:(h + 1) * c] * nw_ref[...] * _silu(zz)

    @pl.when(i == pl.num_programs(1) - 1)
    def _():
        s_out_ref[...] = s_ref[...]


def _gdn_prompt(qkvc, zc, gcol, conv_w, onorm_w, n_rows_out, batch, seq, n_heads, dk, dv, col_g, col_beta, tb):
    nb = seq // tb
    wq = qkvc.shape[1]
    wz = zc.shape[1]
    kern = functools.partial(_gdn_kernel, n_heads=n_heads, dk=dk, dv=dv, col_g=col_g, col_beta=col_beta)
    return pl.pallas_call(
        kern,
        grid=(batch, nb),
        in_specs=[pl.BlockSpec((tb, wq), lambda b, i: (b * nb + i, 0)),
                  pl.BlockSpec((SUBLANES, wq), lambda b, i: (jnp.maximum((b * nb + i) * (tb // SUBLANES) - 1, 0), 0)),
                  pl.BlockSpec((tb, wz), lambda b, i: (b * nb + i, 0)),
                  pl.BlockSpec((tb, LANES), lambda b, i: (b * nb + i, 0)),
                  pl.BlockSpec((CONV_W, wq), lambda b, i: (0, 0)),
                  pl.BlockSpec((1, dv), lambda b, i: (0, 0))],
        out_specs=[pl.BlockSpec((tb, wz), lambda b, i: (b * nb + i, 0)),
                   pl.BlockSpec((None, n_heads, dk, dv), lambda b, i: (b, 0, 0, 0))],
        out_shape=[jax.ShapeDtypeStruct((n_rows_out, wz), F32),
                   jax.ShapeDtypeStruct((batch, n_heads, dk, dv), F32)],
        scratch_shapes=[pltpu.VMEM((n_heads, dk, dv), F32), pltpu.VMEM((SUBLANES + tb, wq), F32)],
        compiler_params=_cparams(("parallel", "arbitrary")),
        name="gdn_prompt",
    )(qkvc, qkvc, zc, gcol, conv_w, onorm_w)


def _fox_decode_kernel(pt_ref, q_ref, kn_ref, vn_ref, lfn_ref, *refs, layer, n_pages, head_dim):
    lf_refs = refs[:n_pages]
    k_hbm, v_hbm, o_ref, kbuf, vbuf, sem = refs[n_pages:]
    b = pl.program_id(0)
    slot = b % 2
    _, _, n_heads, _, page = kbuf.shape

    def start(seq, s):
        for j in range(n_pages):
            pg = pt_ref[seq * n_pages + j]
            pltpu.make_async_copy(k_hbm.at[layer, pg], kbuf.at[s, j], sem.at[0, s]).start()
            pltpu.make_async_copy(v_hbm.at[layer, pg], vbuf.at[s, j], sem.at[1, s]).start()

    @pl.when(b == 0)
    def _():
        start(0, 0)

    @pl.when(b + 1 < pl.num_programs(0))
    def _():
        start(b + 1, 1 - slot)

    pltpu.make_async_copy(k_hbm.at[layer, pl.ds(0, n_pages)], kbuf.at[slot], sem.at[0, slot]).wait()
    pltpu.make_async_copy(v_hbm.at[layer, pl.ds(0, n_pages)], vbuf.at[slot], sem.at[1, slot]).wait()

    qt = q_ref[...] * head_dim ** -0.5
    knt = kn_ref[...]
    vnt = vn_ref[...]
    lfn = lfn_ref[...]
    r = lax.broadcasted_iota(jnp.int32, (page, page), 0)
    c = lax.broadcasted_iota(jnp.int32, (page, page), 1)
    later01 = jnp.where(r > c, 1.0, 0.0).astype(BF16)

    later = jnp.zeros((n_heads, 1), F32)
    rps = [None] * n_pages
    for j in reversed(range(n_pages)):
        lf = lf_refs[j][...]
        rps[j] = _dot_exact_rhs01(lf, later01) + later
        later = later + jnp.sum(lf, axis=-1, keepdims=True)

    for h in range(n_heads):
        qcol = qt[:, h:h + 1]
        qb = jnp.broadcast_to(qcol, (head_dim, page))
        s_new = jnp.sum(qcol * knt[:, h:h + 1], axis=0, keepdims=True) - lfn[:, h:h + 1]
        m = s_new
        scores = []
        for j in range(n_pages):
            s = jnp.sum(kbuf[slot, j, h] * qb, axis=0, keepdims=True) + rps[j][h:h + 1, :]
            scores.append(s)
            m = jnp.maximum(m, jnp.max(s, axis=-1, keepdims=True))
        p_new = jnp.exp(s_new - m)
        denom = p_new
        acc = jnp.zeros((head_dim, page), F32)
        for j in range(n_pages):
            p = jnp.exp(scores[j] - m)
            denom = denom + jnp.sum(p, axis=-1, keepdims=True)
            acc = acc + vbuf[slot, j, h] * p
        o = jnp.sum(acc, axis=-1, keepdims=True) + p_new * vnt[:, h:h + 1]
        o_ref[:, h:h + 1] = o / denom


def _fox_decode(layer, page_table, qt, knt, vnt, lfn, cache_kt, cache_vt, lft):
    bd, n_pages = page_table.shape
    _, _, n_heads, head_dim, page = cache_kt.shape
    kern = functools.partial(_fox_decode_kernel, layer=layer, n_pages=n_pages, head_dim=head_dim)
    col = pl.BlockSpec((None, head_dim, n_heads), lambda b, pt: (b, 0, 0))

    def lf_spec(j):
        return pl.BlockSpec((None, None, n_heads, page), lambda b, pt: (layer, pt[b * n_pages + j], 0, 0))

    in_specs = ([col, col, col, pl.BlockSpec((None, 1, n_heads), lambda b, pt: (b, 0, 0))]
                + [lf_spec(j) for j in range(n_pages)]
                + [pl.BlockSpec(memory_space=pl.ANY), pl.BlockSpec(memory_space=pl.ANY)])
    return pl.pallas_call(
        kern,
        grid_spec=pltpu.PrefetchScalarGridSpec(
            num_scalar_prefetch=1, grid=(bd,), in_specs=in_specs,
            out_specs=pl.BlockSpec((None, head_dim, n_heads), lambda b, pt: (b, 0, 0)),
            scratch_shapes=[pltpu.VMEM((2, n_pages, n_heads, head_dim, page), F32),
                            pltpu.VMEM((2, n_pages, n_heads, head_dim, page), F32),
                            pltpu.SemaphoreType.DMA((2, 2))]),
        out_shape=jax.ShapeDtypeStruct((bd, head_dim, n_heads), F32),
        compiler_params=_cparams(("arbitrary",)),
        name="fox_decode",
    )(page_table.reshape(-1), qt, knt, vnt, lfn, *([lft] * n_pages), cache_kt, cache_vt)


def _pool_decode_kernel(st_ref, x_ref, w_ref, scale_ref, o_ref, new_ref, *, pos0):
    width = x_ref.shape[1]
    n_buf = st_ref.shape[1] // width
    for g, w in enumerate(POOL_WINDOWS):
        sl = pl.ds(g * LANES, LANES)
        win = x_ref[:, sl]
        for j in range(1, w):
            win = win + st_ref[:, pl.ds((n_buf - j) * width + g * LANES, LANES)]
        z = win / float(min(pos0 + 1, w)) - x_ref[:, sl]
        o_ref[:, sl] = _dot(z.astype(BF16), w_ref[g]) * scale_ref[:, sl]
    new_ref[:, 0:(n_buf - 1) * width] = st_ref[:, width:]
    new_ref[:, (n_buf - 1) * width:] = x_ref[...]


def _pool_decode(state_flat, x, w_grp, scale, pos0):
    bd, width = x.shape
    kern = functools.partial(_pool_decode_kernel, pos0=pos0)
    full = lambda a: pl.BlockSpec(a.shape, lambda i: (0,) * a.ndim)
    return pl.pallas_call(
        kern,
        grid=(1,),
        in_specs=[full(state_flat), full(x), full(w_grp), full(scale)],
        out_specs=[full(x), full(state_flat)],
        out_shape=[jax.ShapeDtypeStruct(x.shape, F32), jax.ShapeDtypeStruct(state_flat.shape, F32)],
        compiler_params=_cparams(("arbitrary",)),
        name="pool_decode",
    )(state_flat, x, w_grp, scale)


def _gdn_decode_kernel(x_ref, cs_ref, z_ref, g_ref, s_ref, cw_ref, nw_ref, o_ref, nc_ref, so_ref,
                       *, n_heads, dk, dv, col_g, col_beta):
    x = x_ref[...]
    cs = cs_ref[...]
    y = x * cw_ref[CONV_W - 1:CONV_W, :]
    for j in range(CONV_W - 1):
        y = y + cs[j:j + 1, :] * cw_ref[j:j + 1, :]
    qkv = _silu(y)
    nc_ref[0:CONV_W - 2, :] = cs[1:, :]
    nc_ref[CONV_W - 2:CONV_W - 1, :] = x
    gates = g_ref[...]
    for h in range(n_heads):
        q = qkv[:, h * dk:(h + 1) * dk]
        k = qkv[:, (n_heads + h) * dk:(n_heads + h + 1) * dk]
        v = qkv[:, 2 * n_heads * dk + h * dv:2 * n_heads * dk + (h + 1) * dv]
        q = q * lax.rsqrt(jnp.sum(q * q, axis=-1, keepdims=True) + L2_EPS) * dk ** -0.5
        k = k * lax.rsqrt(jnp.sum(k * k, axis=-1, keepdims=True) + L2_EPS)
        kc = jnp.transpose(jnp.broadcast_to(k, (dk, dk)))
        qc = jnp.transpose(jnp.broadcast_to(q, (dk, dk)))
        a = jnp.exp(gates[:, col_g + h:col_g + h + 1])
        beta = gates[:, col_beta + h:col_beta + h + 1]
        s = s_ref[h] * a
        u = (v - jnp.sum(kc * s, axis=0, keepdims=True)) * beta
        s = s + kc * u
        so_ref[h] = s
        o = jnp.sum(qc * s, axis=0, keepdims=True)
        o = o * lax.rsqrt(jnp.mean(o * o, axis=-1, keepdims=True) + RMS_EPS)
        o_ref[:, pl.ds(h * dv, dv)] = o * nw_ref[...] * _silu(z_ref[:, pl.ds(h * dv, dv)])


def _gdn_decode(x3, conv_state, z3, g3, s_state, conv_w, onorm_w, col_g, col_beta):
    bd, n_heads, dk, dv = s_state.shape
    wq = x3.shape[-1]
    wz = z3.shape[-1]
    kern = functools.partial(_gdn_decode_kernel, n_heads=n_heads, dk=dk, dv=dv, col_g=col_g, col_beta=col_beta)
    per_seq = lambda a: pl.BlockSpec((None,) + a.shape[1:], lambda b: (b,) + (0,) * (a.ndim - 1))
    full = lambda a: pl.BlockSpec(a.shape, lambda b: (0,) * a.ndim)
    return pl.pallas_call(
        kern,
        grid=(bd,),
        in_specs=[per_seq(x3), per_seq(conv_state), per_seq(z3), per_seq(g3), per_seq(s_state),
                  full(conv_w), full(onorm_w)],
        out_specs=[per_seq(z3), per_seq(conv_state), per_seq(s_state)],
        out_shape=[jax.ShapeDtypeStruct(z3.shape, F32), jax.ShapeDtypeStruct(conv_state.shape, F32),
                   jax.ShapeDtypeStruct(s_state.shape, F32)],
        compiler_params=_cparams(("parallel",)),
        name="gdn_decode",
    )(x3, conv_state, z3, g3, s_state, conv_w, onorm_w)


def _merge_kernel(oa_ref, ob_ref, oc_ref, gl_ref, x_ref, wb_ref, wo_ref, g_ref, b_ref, rw_ref, rb_ref,
                  x1_ref, x1t_ref, gate_ref, idx_ref, rank_ref, counts_ref, cnt_ref, *, alpha, n_experts):
    d = x_ref.shape[1]
    m = None
    for bi, o_ref in enumerate((oa_ref, ob_ref, oc_ref)):
        proj = _dot(o_ref[...].astype(BF16), wb_ref[bi])
        term = _sigmoid(gl_ref[:, pl.ds(bi * d, d)]) * proj
        m = term if m is None else m + term
    h = alpha * x_ref[...] + _dot(m.astype(BF16), wo_ref[...])
    x1 = _layer_norm(h, g_ref[...], b_ref[...])
    x1_ref[...] = x1
    _rows_to_tiles(x1t_ref, x1, x1.shape[0], d // LANES)
    logits = _dot_hi(x1, rw_ref[...]) + rb_ref[...]
    lane = lax.broadcasted_iota(jnp.int32, logits.shape, 1)
    lg = jnp.where(lane < n_experts, logits, NEG_BIG)
    vals = []
    picks = []
    gate = jnp.zeros(logits.shape, F32)
    idx = jnp.zeros(logits.shape, jnp.int32)
    for k in range(TOP_K):
        mx = jnp.max(lg, axis=-1, keepdims=True)
        ix = jnp.min(jnp.where(lg == mx, lane, LANES), axis=-1, keepdims=True)
        vals.append(mx)
        picks.append(lane == ix)
        idx = jnp.where(lane == k, ix, idx)
        lg = jnp.where(lane == ix, NEG_BIG, lg)
    exps = [jnp.exp(v - vals[0]) for v in vals]
    tot = exps[0]
    for e in exps[1:]:
        tot = tot + e
    for k in range(TOP_K):
        gate = jnp.where(lane == k, exps[k] / tot, gate)
    gate_ref[...] = gate
    idx_ref[...] = idx

    @pl.when(pl.program_id(0) == 0)
    def _():
        cnt_ref[...] = jnp.zeros(cnt_ref.shape, F32)

    tm = logits.shape[0]
    chosen = jnp.zeros(logits.shape, F32)
    for pk in picks:
        chosen = jnp.where(pk, 1.0, chosen)
    r = lax.broadcasted_iota(jnp.int32, (tm, tm), 0)
    c = lax.broadcasted_iota(jnp.int32, (tm, tm), 1)
    before = _dot(jnp.where(c < r, 1.0, 0.0).astype(BF16), chosen.astype(BF16)) + cnt_ref[...]
    rank = jnp.zeros(logits.shape, F32)
    for k, pk in enumerate(picks):
        rank = jnp.where(lane == k, jnp.sum(jnp.where(pk, before, 0.0), axis=-1, keepdims=True), rank)
    rank_ref[...] = rank.astype(jnp.int32)
    cnt_ref[...] = cnt_ref[...] + jnp.sum(chosen, axis=0, keepdims=True)
    counts_ref[...] = cnt_ref[...]


def _merge(oa, ob, oc, gl, x, wb, wo, ln_g, ln_b, rw, rb, alpha, n_experts, tm):
    nt, d = x.shape
    w = oa.shape[1]
    kern = functools.partial(_merge_kernel, alpha=alpha, n_experts=n_experts)
    rows = lambda width: pl.BlockSpec((tm, width), lambda i: (i, 0))
    full = lambda a: pl.BlockSpec(a.shape, lambda i: (0,) * a.ndim)
    return pl.pallas_call(
        kern,
        grid=(nt // tm,),
        in_specs=[rows(w), rows(w), rows(w), rows(gl.shape[1]), rows(d), full(wb), full(wo), full(ln_g),
                  full(ln_b), full(rw), full(rb)],
        out_specs=[rows(d), pl.BlockSpec((tm * (d // LANES), LANES), lambda i: (i, 0)), rows(LANES), rows(LANES),
                   rows(LANES), pl.BlockSpec((1, LANES), lambda i: (0, 0))],
        out_shape=[jax.ShapeDtypeStruct((nt, d), F32), jax.ShapeDtypeStruct((nt * (d // LANES), LANES), F32),
                   jax.ShapeDtypeStruct((nt, LANES), F32),
                   jax.ShapeDtypeStruct((nt, LANES), jnp.int32), jax.ShapeDtypeStruct((nt, LANES), jnp.int32),
                   jax.ShapeDtypeStruct((1, LANES), F32)],
        scratch_shapes=[pltpu.VMEM((1, LANES), F32)],
        compiler_params=_cparams(("arbitrary",)),
        name="merge_ln_router",
    )(oa, ob, oc, gl, x, wb, wo, ln_g, ln_b, rw, rb)


def _moe_kernel(blk_e_ref, tok0_ref, tok1_ref, tok_next_ref, dst_prev_ref, x_hbm, wu_ref, bu_ref, wd_ref, bd_ref,
                out_hbm, xs0_ref, xs1_ref, xs2_ref, ys0_ref, ys1_ref, ys2_ref, wub_ref, wdb_ref, gsem, ssem,
                *, blk):
    i = pl.program_id(0)
    ds = xs0_ref.shape[0] // blk
    dff = wd_ref.shape[0]
    xs = (xs0_ref, xs1_ref, xs2_ref)
    ys = (ys0_ref, ys1_ref, ys2_ref)

    def tile(ref, start):
        return ref.at[pl.ds(pl.multiple_of(start, ds), ds), :]

    def gather(idx_ref, s):
        for r in range(blk):
            pltpu.make_async_copy(tile(x_hbm, idx_ref[0, r]), xs[s].at[pl.ds(r * ds, ds), :], gsem.at[s]).start()

    def wait_gather(s):
        pltpu.make_async_copy(x_hbm.at[pl.ds(0, blk * ds), :], xs[s], gsem.at[s]).wait()

    def wait_scatter(s):
        pltpu.make_async_copy(ys[s], out_hbm.at[pl.ds(0, blk * ds), :], ssem.at[s]).wait()

    @pl.when(i == 0)
    def _():
        for y_ref in ys:
            y_ref[...] = jnp.zeros(y_ref.shape, F32)
        n_real = out_hbm.shape[0] - 2 * blk * ds
        for s in range(2):
            cp = pltpu.make_async_copy(ys[s], out_hbm.at[pl.ds(n_real + s * blk * ds, blk * ds), :], ssem.at[s])
            cp.start()
            cp.wait()
        gather(tok0_ref, 0)
        gather(tok1_ref, 1)

    @pl.when(jnp.logical_or(i == 0, blk_e_ref[i] != blk_e_ref[jnp.maximum(i - 1, 0)]))
    def _():
        wub_ref[...] = wu_ref[...].astype(BF16)
        wdb_ref[...] = wd_ref[...].astype(BF16)

    def step(slot):
        nxt = (slot + 2) % 3
        wait_gather(slot)

        @pl.when(i >= 2)
        def _():
            wait_scatter(slot)

        xb = _tiles_to_rows(xs[slot], blk, ds).astype(BF16)
        hb = _dot(xb, wub_ref[...]) + bu_ref[...]
        gt = jnp.minimum(hb[:, :dff], SWIGLU_LIMIT)
        up = jnp.clip(hb[:, dff:], -SWIGLU_LIMIT, SWIGLU_LIMIT)
        act = (up + 1.0) * (gt * _sigmoid(SWIGLU_ALPHA * gt))
        _rows_to_tiles(ys[slot], _dot(act.astype(BF16), wdb_ref[...]) + bd_ref[...], blk, ds)

        gather(tok_next_ref, nxt)
        for r in range(blk):
            pltpu.make_async_copy(ys[nxt].at[pl.ds(r * ds, ds), :], tile(out_hbm, dst_prev_ref[0, r]),
                                  ssem.at[nxt]).start()

        @pl.when(i == pl.num_programs(0) - 1)
        def _():
            for o in ((slot + 1) % 3, nxt):
                wait_gather(o)
                wait_scatter(o)

    for s in range(3):
        pl.when(i % 3 == s)(functools.partial(step, s))


def _moe(x1t, blk_e, row_tok, row_dst, wu, bu, wd, bd, n_out_rows):
    n_blk = row_tok.shape[0]
    blk = row_tok.shape[-1]
    n_exp, d, dff2 = wu.shape
    ds = d // LANES
    assert n_blk >= 2
    idx_spec = lambda f: pl.BlockSpec((None, 1, blk), f, memory_space=pltpu.SMEM)
    buf = pltpu.VMEM((blk * ds, LANES), F32)
    return pl.pallas_call(
        functools.partial(_moe_kernel, blk=blk),
        grid_spec=pltpu.PrefetchScalarGridSpec(
            num_scalar_prefetch=1, grid=(n_blk + 1,),
            in_specs=[idx_spec(lambda i, be: (0, 0, 0)),
                      idx_spec(lambda i, be: (1, 0, 0)),
                      idx_spec(lambda i, be: (jnp.minimum(i + 2, n_blk - 1), 0, 0)),
                      idx_spec(lambda i, be: (i, 0, 0)),
                      pl.BlockSpec(memory_space=pl.ANY),
                      pl.BlockSpec((None, d, dff2), lambda i, be: (be[i], 0, 0)),
                      pl.BlockSpec((None, 1, dff2), lambda i, be: (be[i], 0, 0)),
                      pl.BlockSpec((None, dff2 // 2, d), lambda i, be: (be[i], 0, 0)),
                      pl.BlockSpec((None, 1, d), lambda i, be: (be[i], 0, 0))],
            out_specs=pl.BlockSpec(memory_space=pl.ANY),
            scratch_shapes=[buf, buf, buf, buf, buf, buf,
                            pltpu.VMEM((d, dff2), BF16), pltpu.VMEM((dff2 // 2, d), BF16),
                            pltpu.SemaphoreType.DMA((3,)), pltpu.SemaphoreType.DMA((3,))]),
        out_shape=jax.ShapeDtypeStruct((n_out_rows * ds, LANES), F32),
        compiler_params=_cparams(("arbitrary",)),
        name="moe_experts",
    )(blk_e, row_tok, row_tok, row_tok, row_dst, x1t, wu, bu, wd, bd)


def _route(idx, rank, counts, n_experts, nt, tile_rows):
    tk = nt * TOP_K
    pcounts = (counts + MOE_BLOCK - 1) // MOE_BLOCK * MOE_BLOCK
    pend = jnp.cumsum(pcounts)
    pstart = pend - pcounts
    experts = jnp.arange(n_experts, dtype=jnp.int32)
    dest = rank + jnp.sum(jnp.where(idx[:, :, None] == experts, pstart, 0), axis=-1)
    n_rows = -(-(tk + n_experts * (MOE_BLOCK - 1)) // MOE_BLOCK) * MOE_BLOCK
    n_blk = n_rows // MOE_BLOCK
    row_pair = jnp.full((n_rows,), -1, jnp.int32).at[dest.reshape(-1)].set(jnp.arange(tk, dtype=jnp.int32))
    rows = jnp.arange(n_rows, dtype=jnp.int32)
    dump = TOP_K * nt + (rows // MOE_BLOCK % 2) * MOE_BLOCK + rows % MOE_BLOCK
    row_tok = jnp.where(row_pair >= 0, row_pair // TOP_K, 0)
    row_dst = jnp.where(row_pair >= 0, (row_pair % TOP_K) * nt + row_pair // TOP_K, dump)
    row_dst = jnp.concatenate([dump[MOE_BLOCK:2 * MOE_BLOCK], row_dst])
    blk_start = jnp.arange(n_blk, dtype=jnp.int32) * MOE_BLOCK
    blk_e = jnp.minimum(jnp.sum((pend[None, :] <= blk_start[:, None]).astype(jnp.int32), axis=1), n_experts - 1)
    blk_e = jnp.concatenate([blk_e, blk_e[-1:]])
    return (blk_e, (row_tok * tile_rows).reshape(n_blk, 1, MOE_BLOCK),
            (row_dst * tile_rows).reshape(n_blk + 1, 1, MOE_BLOCK))


def _ple_kernel(x1_ref, y0_ref, y1_ref, y2_ref, y3_ref, gate_ref, p_ref, wg_ref, wp_ref, g_ref, b_ref, o_ref, *, alpha):
    gate = gate_ref[...]
    tm, d = x1_ref.shape
    f = None
    for k, y_ref in enumerate((y0_ref, y1_ref, y2_ref, y3_ref)):
        term = gate[:, k:k + 1] * _tiles_to_rows(y_ref, tm, d // LANES)
        f = term if f is None else f + term
    h = alpha * x1_ref[...] + f
    gpe = _sigmoid(_dot(h.astype(BF16), wg_ref[...])) * _dot(p_ref[...].astype(BF16), wp_ref[...])
    o_ref[...] = _layer_norm(h + gpe, g_ref[...], b_ref[...])


def _ple(x1, out4, gate, p, wg, wp, ln_g, ln_b, alpha, tm):
    nt, d = x1.shape
    nb = nt // tm
    kern = functools.partial(_ple_kernel, alpha=alpha)
    rows = lambda width: pl.BlockSpec((tm, width), lambda i: (i, 0))
    full = lambda a: pl.BlockSpec(a.shape, lambda i: (0,) * a.ndim)
    slot = lambda k: pl.BlockSpec((tm * (d // LANES), LANES), lambda i: (k * nb + i, 0))
    return pl.pallas_call(
        kern,
        grid=(nb,),
        in_specs=[rows(d), slot(0), slot(1), slot(2), slot(3), rows(LANES), rows(p.shape[1]), full(wg), full(wp),
                  full(ln_g), full(ln_b)],
        out_specs=rows(d),
        out_shape=jax.ShapeDtypeStruct((nt, d), F32),
        compiler_params=_cparams(("parallel",)),
        name="combine_ple_ln",
    )(x1, out4, out4, out4, out4, gate, p, wg, wp, ln_g, ln_b)


def _row_tile(n, cap):
    best = None
    for t in range(LANES, cap + 1, LANES):
        if n % t == 0:
            best = t
    assert best is not None, n
    return best


def kernel(x_prompt, x_sample, cache_k, cache_v, cache_logf, state_pool, state_conv, state_delta, page_table,
           p_prompt, p_sample, w_in, b_f, w_grp, pool_scale, conv_w, a_log, dt_bias, onorm_w, w_branch, w_o,
           ln1_g, ln1_b, router_w, router_b, w_up, b_up, w_down, b_down, w_pe, w_peg, ln2_g, ln2_b):
    batch, seq, d = x_prompt.shape
    bd, dec_seq, _ = x_sample.shape
    assert dec_seq == 1
    depth = w_in.shape[0]
    _, _, page, n_heads, head_dim = cache_k.shape
    width = n_heads * head_dim
    _, _, gh, dk, dv = state_delta.shape
    gqkv = state_conv.shape[3]
    n_branch = w_branch.shape[1]
    n_exp = router_w.shape[2]
    ple_dim = p_prompt.shape[3]
    n_buf = state_pool.shape[2]
    past = page_table.shape[1] * page
    alpha = (2 * depth) ** 0.25
    n_p = batch * seq
    nt = n_p + bd
    tm = _row_tile(nt, 512)

    sizes = (width, width, width, n_heads, width, gqkv, gh, gh, width, n_branch * d)
    offs = [0]
    for s in sizes:
        offs.append(offs[-1] + s)
    o_q, _, _, o_f, o_ub, o_c, o_a, o_b, o_z, o_gl, o_end = offs
    assert o_end == w_in.shape[2]
    col_g, col_beta = n_heads, n_heads + gh

    x = jnp.concatenate([x_prompt.reshape(n_p, d), x_sample.reshape(bd, d)], axis=0)
    lft = jnp.swapaxes(cache_logf, 2, 3)
    cache_kt = jnp.transpose(cache_k, (0, 1, 3, 4, 2))
    cache_vt = jnp.transpose(cache_v, (0, 1, 3, 4, 2))
    rows_p, rows_s = [], []
    for l in range(depth):
        wt = jnp.swapaxes(w_in[l], 0, 1)
        rows = lambda a, n: wt[a:a + n].astype(BF16)
        wt_small = jnp.concatenate([wt[o_f:o_f + n_heads], wt[o_a:o_a + gh], wt[o_b:o_b + gh]], axis=0)
        bias = jnp.concatenate([b_f[l], dt_bias[l], jnp.zeros((gh,), F32)])
        alog = jnp.concatenate([jnp.zeros((n_heads,), F32), a_log[l], jnp.zeros((gh,), F32)])
        qa, ka, va, ub, qkvc, zc = _project(x, [rows(o_q, width), rows(o_q + width, width),
                                                rows(o_q + 2 * width, width), rows(o_ub, width),
                                                rows(o_c, gqkv), rows(o_z, width)], tm)
        gl, = _project(x, [rows(o_gl, n_branch * d)], tm)
        kt, vt = _kv_transposed(x, rows(o_q + width, width), rows(o_q + 2 * width, width), batch, seq,
                                min(seq, 512))
        gcol, gt = _small_proj(x, wt_small, bias, alog, n_heads, gh, tm)
        ct = _cumsum_rows(gt, batch, seq)

        wg_bf = w_grp[l].astype(BF16)
        pscale = pool_scale[l].reshape(1, width)
        onw = onorm_w[l].reshape(1, dv)
        oa = _fox_prompt(qa, ka, va, ct, n_p, batch, seq, n_heads, head_dim, min(seq, 512))
        ob = _pool_prompt(ub, wg_bf, pscale, n_p, batch, seq, min(seq, 256))
        oc, s_p = _gdn_prompt(qkvc, zc, gcol, conv_w[l], onw, n_p, batch, seq, gh, dk, dv, col_g, col_beta,
                              min(seq, 4 * GDN_CHUNK))

        q3 = qa[n_p:].reshape(bd, n_heads, head_dim)
        kn3 = ka[n_p:].reshape(bd, n_heads, head_dim)
        vn3 = va[n_p:].reshape(bd, n_heads, head_dim)
        lfn = gcol[n_p:, :n_heads].reshape(bd, 1, n_heads)
        to_cols = lambda a: jnp.swapaxes(a, 1, 2)
        oa_s = to_cols(_fox_decode(l, page_table, to_cols(q3), to_cols(kn3), to_cols(vn3), lfn,
                                   cache_kt, cache_vt, lft)).reshape(bd, width)
        ob_s, pool_new = _pool_decode(state_pool[l].reshape(bd, n_buf * width), ub[n_p:], wg_bf, pscale, past)
        oc_s, conv_new, s_new = _gdn_decode(qkvc[n_p:].reshape(bd, 1, gqkv), state_conv[l],
                                            zc[n_p:].reshape(bd, 1, width), gcol[n_p:].reshape(bd, 1, LANES),
                                            state_delta[l], conv_w[l], onw, col_g, col_beta)
        oa = jnp.concatenate([oa, oa_s], axis=0)
        ob = jnp.concatenate([ob, ob_s], axis=0)
        oc = jnp.concatenate([oc, oc_s.reshape(bd, width)], axis=0)

        rw = jnp.zeros((d, LANES), F32).at[:, :n_exp].set(router_w[l])
        rb = jnp.zeros((1, LANES), F32).at[0, :n_exp].set(router_b[l])
        x1, x1t, gate, idx, rank, counts = _merge(oa, ob, oc, gl, x, w_branch[l].astype(BF16), w_o[l].astype(BF16),
                                                  ln1_g[l].reshape(1, d), ln1_b[l].reshape(1, d), rw, rb, alpha,
                                                  n_exp, tm)
        blk_e, row_tok, row_dst = _route(idx[:, :TOP_K], rank[:, :TOP_K], counts[0, :n_exp].astype(jnp.int32),
                                         n_exp, nt, d // LANES)
        out4 = _moe(x1t, blk_e, row_tok, row_dst, w_up[l], b_up[l].reshape(n_exp, 1, -1),
                    w_down[l], b_down[l].reshape(n_exp, 1, -1), TOP_K * nt + 2 * MOE_BLOCK)
        p = jnp.concatenate([p_prompt[l].reshape(n_p, ple_dim), p_sample[l].reshape(bd, ple_dim)], axis=0)
        x = _ple(x1, out4, gate, p, w_peg[l].astype(BF16), w_pe[l].astype(BF16), ln2_g[l].reshape(1, d),
                 ln2_b[l].reshape(1, d), alpha, tm)

        from_cache_layout = lambda a: jnp.transpose(a.reshape(batch, n_heads, head_dim, seq), (0, 3, 1, 2))
        rows_p.append((from_cache_layout(kt), from_cache_layout(vt),
                       gcol[:n_p, :n_heads].reshape(batch, seq, n_heads),
                       ub[:n_p].reshape(batch, seq, width)[:, seq - n_buf:],
                       qkvc[:n_p].reshape(batch, seq, gqkv)[:, seq - (CONV_W - 1):],
                       s_p))
        rows_s.append((kn3.reshape(bd, 1, n_heads, head_dim), vn3.reshape(bd, 1, n_heads, head_dim),
                       gcol[n_p:, :n_heads].reshape(bd, 1, n_heads), pool_new.reshape(bd, n_buf, width),
                       conv_new, s_new))

    def stk(rows, i):
        return jnp.stack([r[i] for r in rows])

    return (x[:n_p].reshape(batch, seq, d), x[n_p:].reshape(bd, 1, d),
            stk(rows_p, 0), stk(rows_p, 1), stk(rows_p, 2), stk(rows_p, 3), stk(rows_p, 4), stk(rows_p, 5),
            stk(rows_s, 0), stk(rows_s, 1), stk(rows_s, 2), stk(rows_s, 3), stk(rows_s, 4), stk(rows_s, 5))
```
